```python
import jax, jax.numpy as jnp
from jax import lax
import numpy as np

D_MODEL = 1024
BATCH = 8
SEQ = 2048
DEPTH = 4
DEC_BATCH = 32
DEC_SEQ = 16
PAST_LEN = 4096

CHUNK = 64
N_META = 16
D_MIX = D_MODEL
D_RWKV = D_MIX // 2
D_CONV = D_MIX - D_RWKV
HEAD_DIM = 64
N_HEADS = D_RWKV // HEAD_DIM
LORA_W = 64
LORA_A = 64
LORA_G = 160
CONV_WIDTH = 31
D_FF = 4 * D_MODEL
P_RWKV = 3 * D_RWKV + LORA_W + LORA_A + LORA_G
P_IN = P_RWKV + 2 * D_CONV
NORM_EPS = 1e-6
LN_EPS = 1e-5
GN_EPS = 64e-5
DECAY_SCALE = 0.606531

kernel_name = "hymba_rwkv7_conformer_stream_step"


def rms_norm(x, g):
    xf = x.astype(jnp.float32)
    y = xf * lax.rsqrt(jnp.mean(xf * xf, axis=-1, keepdims=True) + NORM_EPS)
    return (y * g.astype(jnp.float32)).astype(x.dtype)


def layer_norm_f32(x, g, b, eps):
    xf = x.astype(jnp.float32)
    mu = jnp.mean(xf, axis=-1, keepdims=True)
    var = jnp.mean(jnp.square(xf - mu), axis=-1, keepdims=True)
    return (xf - mu) * lax.rsqrt(var + eps) * g.astype(jnp.float32) + b.astype(jnp.float32)


def wkv7_scan(S0, r, w, k, v, kk, b):
    def step(S, inp):
        r_t, w_t, k_t, v_t, kk_t, b_t = inp
        sa = jnp.einsum('bhvk,bhk->bhv', S, kk_t)
        S = S * w_t[:, :, None, :] - sa[..., None] * b_t[:, :, None, :] + v_t[..., None] * k_t[:, :, None, :]
        return S, jnp.einsum('bhvk,bhk->bhv', S, r_t)
    xs = (jnp.moveaxis(r, 1, 0), jnp.moveaxis(w, 1, 0), jnp.moveaxis(k, 1, 0),
          jnp.moveaxis(v, 1, 0), jnp.moveaxis(kk, 1, 0), jnp.moveaxis(b, 1, 0))
    S, o = lax.scan(step, S0, xs)
    return S, jnp.moveaxis(o, 0, 1)


def trunk_layer(h, wkv0, shift0, conv0, norm_mix, w_in, mu_shift, w0, lora_w, a0, lora_a, lora_g,
                k_k, k_a, r_k, gn_g, gn_b, conv_w, conv_b, cln_g, cln_b, w_out, norm_ffn, w_up, w_down):
    B, T, _ = h.shape
    xn = rms_norm(h, norm_mix)
    proj = xn @ w_in
    p_r = proj[..., :P_RWKV]
    p_c = proj[..., P_RWKV:]

    prev = jnp.concatenate([shift0[:, None].astype(p_r.dtype), p_r[:, :-1]], axis=1)
    xs = (p_r + (prev - p_r) * mu_shift).astype(jnp.float32)
    o1 = D_RWKV
    o2 = 2 * D_RWKV
    o3 = 3 * D_RWKV
    o4 = o3 + LORA_W
    o5 = o4 + LORA_A
    r = xs[..., :o1]
    k = xs[..., o1:o2]
    v = xs[..., o2:o3]
    xw = xs[..., o3:o4]
    xa = xs[..., o4:o5]
    xg = xs[..., o5:]
    w = jnp.exp(-DECAY_SCALE * jax.nn.sigmoid(w0 + jnp.tanh(xw) @ lora_w))
    a = jax.nn.sigmoid(a0 + xa @ lora_a)
    g = jax.nn.sigmoid(xg) @ lora_g

    def heads(t):
        return t.reshape(B, T, N_HEADS, HEAD_DIM)

    kk = heads(k * k_k)
    kk = kk / jnp.maximum(jnp.sqrt(jnp.sum(kk * kk, axis=-1, keepdims=True)), 1e-12)
    k = k * (1.0 + (a - 1.0) * k_a)
    rh, kh, vh = heads(r), heads(k), heads(v)
    S, o = wkv7_scan(wkv0.astype(jnp.float32), rh, heads(w), kh, vh, kk, kk * heads(a))
    o = layer_norm_f32(o, gn_g.reshape(N_HEADS, HEAD_DIM), gn_b.reshape(N_HEADS, HEAD_DIM), GN_EPS)
    o = o + jnp.sum(rh * kh * r_k.astype(jnp.float32), axis=-1, keepdims=True) * vh
    y_r = o.reshape(B, T, D_RWKV) * g

    u = p_c[..., :D_CONV] * jax.nn.sigmoid(p_c[..., D_CONV:])
    full = jnp.concatenate([conv0.astype(u.dtype), u], axis=1)
    c = lax.conv_general_dilated(full, conv_w[:, None, :].astype(u.dtype), (1,), 'VALID',
                                 dimension_numbers=('NWC', 'WIO', 'NWC'),
                                 feature_group_count=D_CONV) + conv_b
    y_c = jax.nn.silu(layer_norm_f32(c, cln_g, cln_b, LN_EPS))

    mix = jnp.concatenate([y_r, y_c], axis=-1).astype(h.dtype)
    h = h + mix @ w_out
    hn = rms_norm(h, norm_ffn)
    h = h + jnp.square(jax.nn.relu(hn @ w_up)) @ w_down
    return h, S.astype(wkv0.dtype), p_r[:, -1], full[:, -(CONV_WIDTH - 1):]


def setup_inputs(seed: int = 0) -> dict:
    key = jax.random.key(seed)
    ks = jax.random.split(key, 32)
    nrm = jax.random.normal
    f32 = jnp.float32
    return {
        "x_prompt": nrm(ks[0], (BATCH, SEQ, D_MODEL), f32),
        "x_sample": nrm(ks[1], (DEC_BATCH, DEC_SEQ, D_MODEL), f32),
        "state_wkv": 0.3 * nrm(ks[2], (DEPTH, DEC_BATCH, N_HEADS, HEAD_DIM, HEAD_DIM), f32),
        "state_shift": nrm(ks[3], (DEPTH, DEC_BATCH, P_RWKV), f32),
        "cache_conv": 0.5 * nrm(ks[4], (DEPTH, DEC_BATCH, CONV_WIDTH - 1, D_CONV), f32),
        "meta_tokens": nrm(ks[5], (N_META, D_MODEL), f32),
        "norm_mix": 1.0 + 0.02 * nrm(ks[6], (DEPTH, D_MODEL), f32),
        "w_in": nrm(ks[7], (DEPTH, D_MODEL, P_IN), f32) * D_MODEL ** -0.5,
        "mu_shift": jax.random.uniform(ks[8], (DEPTH, P_RWKV), f32),
        "w0": nrm(ks[9], (DEPTH, D_RWKV), f32),
        "lora_w": 0.5 * nrm(ks[10], (DEPTH, LORA_W, D_RWKV), f32) * LORA_W ** -0.5,
        "a0": 0.5 * nrm(ks[11], (DEPTH, D_RWKV), f32),
        "lora_a": 0.5 * nrm(ks[12], (DEPTH, LORA_A, D_RWKV), f32) * LORA_A ** -0.5,
        "lora_g": nrm(ks[13], (DEPTH, LORA_G, D_RWKV), f32) * LORA_G ** -0.5,
        "k_k": 0.85 + 0.05 * nrm(ks[14], (DEPTH, D_RWKV), f32),
        "k_a": 1.0 + 0.05 * nrm(ks[15], (DEPTH, D_RWKV), f32),
        "r_k": 0.1 * nrm(ks[16], (DEPTH, N_HEADS, HEAD_DIM), f32),
        "gn_g": 1.0 + 0.02 * nrm(ks[17], (DEPTH, D_RWKV), f32),
        "gn_b": 0.01 * nrm(ks[18], (DEPTH, D_RWKV), f32),
        "conv_w": nrm(ks[19], (DEPTH, CONV_WIDTH, D_CONV), f32) * CONV_WIDTH ** -0.5,
        "conv_b": 0.01 * nrm(ks[20], (DEPTH, D_CONV), f32),
        "cln_g": 1.0 + 0.02 * nrm(ks[21], (DEPTH, D_CONV), f32),
        "cln_b": 0.01 * nrm(ks[22], (DEPTH, D_CONV), f32),
        "w_out": nrm(ks[23], (DEPTH, D_MIX, D_MODEL), f32) * D_MIX ** -0.5,
        "norm_ffn": 1.0 + 0.02 * nrm(ks[24], (DEPTH, D_MODEL), f32),
        "w_up": nrm(ks[25], (DEPTH, D_MODEL, D_FF), f32) * D_MODEL ** -0.5,
        "w_down": nrm(ks[26], (DEPTH, D_FF, D_MODEL), f32) * D_FF ** -0.5,
        "norm_final": 1.0 + 0.02 * nrm(ks[27], (D_MODEL,), f32),
    }


def reference(x_prompt, x_sample, state_wkv, state_shift, cache_conv, meta_tokens, norm_mix, w_in,
              mu_shift, w0, lora_w, a0, lora_a, lora_g, k_k, k_a, r_k, gn_g, gn_b, conv_w, conv_b,
              cln_g, cln_b, w_out, norm_ffn, w_up, w_down, norm_final):
    bp = x_prompt.shape[0]
    dt = x_prompt.dtype
    hp = jnp.concatenate([jnp.broadcast_to(meta_tokens[None].astype(dt), (bp, N_META, D_MODEL)), x_prompt], axis=1)
    wkv_p0 = jnp.zeros((bp, N_HEADS, HEAD_DIM, HEAD_DIM), dt)
    shift_p0 = jnp.zeros((bp, P_RWKV), dt)
    conv_p0 = jnp.zeros((bp, CONV_WIDTH - 1, D_CONV), dt)
    hs = x_sample
    wkv_p, shift_p, conv_p, wkv_s, shift_s, conv_s = [], [], [], [], [], []
    for l in range(DEPTH):
        params = (norm_mix[l], w_in[l], mu_shift[l], w0[l], lora_w[l], a0[l], lora_a[l], lora_g[l],
                  k_k[l], k_a[l], r_k[l], gn_g[l], gn_b[l], conv_w[l], conv_b[l], cln_g[l], cln_b[l],
                  w_out[l], norm_ffn[l], w_up[l], w_down[l])
        hp, s1, s2, s3 = trunk_layer(hp, wkv_p0, shift_p0, conv_p0, *params)
        wkv_p.append(s1)
        shift_p.append(s2)
        conv_p.append(s3)
        hs, s1, s2, s3 = trunk_layer(hs, state_wkv[l], state_shift[l], cache_conv[l], *params)
        wkv_s.append(s1)
        shift_s.append(s2)
        conv_s.append(s3)
    y_prompt = rms_norm(hp, norm_final)[:, N_META:]
    y_sample = rms_norm(hs, norm_final)
    return (y_prompt, y_sample,
            jnp.stack(wkv_p), jnp.stack(shift_p), jnp.stack(conv_p),
            jnp.stack(wkv_s), jnp.stack(shift_s), jnp.stack(conv_s))
```

```python
import functools

import jax
import jax.numpy as jnp
from jax import lax
from jax.experimental import pallas as pl
from jax.experimental.pallas import tpu as pltpu

D_MODEL = 1024
DEPTH = 4
N_META = 16
HEAD_DIM = 64
N_HEADS = 8
D_RWKV = N_HEADS * HEAD_DIM
D_CONV = D_MODEL - D_RWKV
LORA_W = 64
LORA_A = 64
LORA_G = 160
CONV_WIDTH = 31
D_FF = 4 * D_MODEL
P_RWKV = 3 * D_RWKV + LORA_W + LORA_A + LORA_G
NORM_EPS = 1e-6
LN_EPS = 1e-5
GN_EPS = 64e-5
DECAY_SCALE = 0.606531

LANES = 128
P_RWKV_PAD = 15 * LANES
LORA_G_PAD = P_RWKV_PAD - (3 * D_RWKV + LORA_W + LORA_A)
O_WA = 3 * D_RWKV
O_G = O_WA + LORA_W + LORA_A
P_IN_PAD = P_RWKV_PAD + 2 * D_CONV
CONV_HIST = 32
WKV_CHUNK = 64
VMEM_LIMIT = 48 * 1024 * 1024

F32 = jnp.float32
BF16 = jnp.bfloat16


def _dot(a, b):
    return jnp.dot(a, b, preferred_element_type=F32)


def _dot_nt(a, b):
    return lax.dot_general(a, b, (((1,), (1,)), ((), ())), preferred_element_type=F32)


def _dot_tn(a, b):
    return lax.dot_general(a, b, (((0,), (0,)), ((), ())), preferred_element_type=F32)


def _split(x):
    hi = x.astype(BF16)
    lo = (x - hi.astype(F32)).astype(BF16)
    return hi, lo


def _dot3(a, b):
    ah, al = _split(a)
    bh, bl = _split(b)
    return _dot(ah, bh) + (_dot(al, bh) + _dot(ah, bl))


def _rms(x, g):
    return x * lax.rsqrt(jnp.mean(x * x, axis=-1, keepdims=True) + NORM_EPS) * g


def _const_spec(shape):
    return pl.BlockSpec(shape, lambda *_: (0,) * len(shape))


def _proj_kernel(h_ref, g_ref, w_ref, pr_ref, u_ref):
    xn = _rms(h_ref[...], g_ref[...]).astype(BF16)
    pr_ref[...] = _dot(xn, w_ref[:, :P_RWKV_PAD])
    ca = _dot(xn, w_ref[:, P_RWKV_PAD:P_RWKV_PAD + D_CONV])
    cb = _dot(xn, w_ref[:, P_RWKV_PAD + D_CONV:])
    u_ref[...] = ca * jax.nn.sigmoid(cb)


def _proj(h, g, w, tm):
    n = h.shape[0]
    return pl.pallas_call(
        _proj_kernel,
        grid=(n // tm,),
        in_specs=[pl.BlockSpec((tm, D_MODEL), lambda i: (i, 0)),
                  _const_spec((1, D_MODEL)),
                  _const_spec((D_MODEL, P_IN_PAD))],
        out_specs=[pl.BlockSpec((tm, P_RWKV_PAD), lambda i: (i, 0)),
                   pl.BlockSpec((tm, D_CONV), lambda i: (i, 0))],
        out_shape=[jax.ShapeDtypeStruct((n, P_RWKV_PAD), F32),
                   jax.ShapeDtypeStruct((n, D_CONV), F32)],
        compiler_params=pltpu.CompilerParams(dimension_semantics=("parallel",),
                                             vmem_limit_bytes=VMEM_LIMIT),
        name="proj",
    )(h, g, w)


def _head_sum(x):
    outs = []
    for j in range(D_RWKV // LANES):
        xj = x[:, j * LANES:(j + 1) * LANES]
        low = lax.broadcasted_iota(jnp.int32, xj.shape, 1) < HEAD_DIM
        s_lo = jnp.sum(jnp.where(low, xj, 0.0), axis=-1, keepdims=True)
        s_hi = jnp.sum(jnp.where(low, 0.0, xj), axis=-1, keepdims=True)
        outs.append(jnp.where(low, s_lo, s_hi))
    return jnp.concatenate(outs, axis=1)


def _wkv_chunk_head(s0, kk, rt, bt, kt, v, wc, masks):
    strict, incl, eye, offs = masks
    c = kk.shape[0]
    lhs = jnp.concatenate([kk, rt], axis=0)
    gb = _dot_nt(lhs, bt)
    gk = _dot_nt(lhs, kt)
    lb = jnp.where(strict, gb[:c], 0.0)
    lk = jnp.where(strict, gk[:c], 0.0).astype(BF16)
    arb = jnp.where(incl, gb[c:], 0.0).astype(BF16)
    ark = jnp.where(incl, gk[c:], 0.0).astype(BF16)
    t = eye - jnp.where(offs[0], lb, 0.0)
    for off in offs[1:]:
        tb = t.astype(BF16)
        t = t - _dot(tb, _dot(jnp.where(off, lb, 0.0).astype(BF16), tb).astype(BF16))
    x = _dot_nt(lhs, s0.astype(BF16))
    u = _dot3(t, -x[:c] - _dot(lk, v))
    ub = u.astype(BF16)
    o = x[c:] + _dot(arb, ub) + _dot(ark, v)
    s1 = (s0 + _dot_tn(jnp.concatenate([ub, v], axis=0), jnp.concatenate([bt, kt], axis=0))) * wc
    return o, s1


def _chunk_masks(c):
    ti = lax.broadcasted_iota(jnp.int32, (c, c), 0)
    si = lax.broadcasted_iota(jnp.int32, (c, c), 1)
    offs = []
    n = 1
    while n < c:
        same = ((ti ^ si) & ~(2 * n - 1)) == 0
        offs.append(same & ((ti & n) != 0) & ((si & n) == 0))
        n *= 2
    return ti > si, ti >= si, jnp.where(ti == si, 1.0, 0.0).astype(F32), offs


def _mixer_kernel(pr_ref, u_ref, wkv0_ref, shift0_ref, conv0_ref,
                  mu_ref, wwa_ref, w0a0_ref, lg_ref, kk_ref, ka_ref, rk_ref, gng_ref, gnb_ref,
                  cw_ref, cb_ref, clg_ref, clb_ref,
                  mix_ref, wkv_ref, shift_ref, conv_ref,
                  s_ref, carry_ref, ext_ref, kk_s, rt_s, bt_s, kt_s, v_s, wc_s, o_s, *, chunk):
    t_idx = pl.program_id(1)
    tt = pr_ref.shape[1]
    n_chunks = tt // chunk

    @pl.when(t_idx == 0)
    def _():
        s_ref[...] = wkv0_ref[0]
        carry_ref[0:1, :] = shift0_ref[0]
        ext_ref[0:CONV_HIST, :] = conv0_ref[0]

    p = pr_ref[0]
    row = lax.broadcasted_iota(jnp.int32, p.shape, 0)
    prev = jnp.where(row == 0, carry_ref[0:1, :], pltpu.roll(p, 1, 0))
    xs = p + (prev - p) * mu_ref[...]
    carry_ref[0:1, :] = p[tt - 1:tt, :]

    r = xs[:, 0:D_RWKV]
    k = xs[:, D_RWKV:2 * D_RWKV]
    v = xs[:, 2 * D_RWKV:3 * D_RWKV]
    xwa = xs[:, O_WA:O_G]
    lane = lax.broadcasted_iota(jnp.int32, xwa.shape, 1)
    wa = _dot(jnp.where(lane < LORA_W, jnp.tanh(xwa), xwa).astype(BF16), wwa_ref[...]) + w0a0_ref[...]
    logw = -DECAY_SCALE * jax.nn.sigmoid(wa[:, :D_RWKV])
    a = jax.nn.sigmoid(wa[:, D_RWKV:])
    g = _dot(jax.nn.sigmoid(xs[:, O_G:]).astype(BF16), lg_ref[...])

    kk = k * kk_ref[...]
    kk = kk / jnp.maximum(jnp.sqrt(_head_sum(kk * kk)), 1e-12)
    k = k * (1.0 + (a - 1.0) * ka_ref[...])
    bonus = _head_sum(r * k * rk_ref[...]) * v

    ti = lax.broadcasted_iota(jnp.int32, (tt, tt), 0)
    si = lax.broadcasted_iota(jnp.int32, (tt, tt), 1)
    tril = jnp.where((ti >= si) & ((ti ^ si) < chunk), 1.0, 0.0).astype(BF16)
    l_hi, l_lo = _split(logw)
    l_lo2 = (logw - l_hi.astype(F32) - l_lo.astype(F32)).astype(BF16)
    cl = _dot(tril, l_hi) + (_dot(tril, l_lo) + _dot(tril, l_lo2))
    w_inc = jnp.exp(cl)
    w_inv = jnp.exp(-cl)
    kk_s[...] = (kk * jnp.exp(cl - logw)).astype(BF16)
    rt_s[...] = (r * w_inc).astype(BF16)
    bt_s[...] = (kk * a * w_inv).astype(BF16)
    kt_s[...] = (k * w_inv).astype(BF16)
    v_s[...] = v.astype(BF16)
    for c in range(n_chunks):
        wc_s[c * 8:(c + 1) * 8, :] = jnp.broadcast_to(w_inc[(c + 1) * chunk - 1:(c + 1) * chunk, :], (8, D_RWKV))

    masks = _chunk_masks(chunk)

    def chunk_body(c, carry):
        rows = pl.ds(pl.multiple_of(c * chunk, chunk), chunk)
        wc_rows = pl.ds(pl.multiple_of(c * 8, 8), 8)
        for h in range(N_HEADS):
            cols = slice(h * HEAD_DIM, (h + 1) * HEAD_DIM)
            o, s1 = _wkv_chunk_head(s_ref[h], kk_s[rows, cols], rt_s[rows, cols], bt_s[rows, cols],
                                    kt_s[rows, cols], v_s[rows, cols], wc_s[wc_rows, cols][0:1], masks)
            o_s[rows, cols] = o
            s_ref[h] = s1
        return carry

    lax.fori_loop(0, n_chunks, chunk_body, 0)

    o = o_s[...]
    mean = _head_sum(o) * (1.0 / HEAD_DIM)
    oc = o - mean
    var = _head_sum(oc * oc) * (1.0 / HEAD_DIM)
    y_r = (oc * lax.rsqrt(var + GN_EPS) * gng_ref[...] + gnb_ref[...] + bonus) * g
    mix_ref[0, :, 0:D_RWKV] = y_r

    ext_ref[CONV_HIST:CONV_HIST + tt, :] = u_ref[0]
    acc = jnp.zeros((tt, D_CONV), F32) + cb_ref[...]
    for j in range(CONV_WIDTH):
        off = CONV_HIST - (CONV_WIDTH - 1) + j
        acc = acc + ext_ref[off:off + tt, :] * cw_ref[j:j + 1, :]
    m = jnp.mean(acc, axis=-1, keepdims=True)
    cc = acc - m
    cv = jnp.mean(cc * cc, axis=-1, keepdims=True)
    ln = cc * lax.rsqrt(cv + LN_EPS) * clg_ref[...] + clb_ref[...]
    mix_ref[0, :, D_RWKV:D_MODEL] = ln * jax.nn.sigmoid(ln)
    hist = ext_ref[tt:tt + CONV_HIST, :]
    ext_ref[0:CONV_HIST, :] = hist

    @pl.when(t_idx == pl.num_programs(1) - 1)
    def _():
        wkv_ref[0] = s_ref[...]
        shift_ref[0] = p[tt - 1:tt, :]
        conv_ref[0] = hist


def _mixer(pr, u, wkv0, shift0, conv0, lp, tt, chunk):
    b, t, _ = pr.shape
    per = wkv0.shape[0] == b

    def st(nd):
        return (lambda i, j: (i,) + (0,) * (nd - 1)) if per else (lambda i, j: (0,) * nd)

    row = lambda n: _const_spec((1, n))
    kern = functools.partial(_mixer_kernel, chunk=chunk)
    return pl.pallas_call(
        kern,
        grid=(b, t // tt),
        in_specs=[pl.BlockSpec((1, tt, P_RWKV_PAD), lambda i, j: (i, j, 0)),
                  pl.BlockSpec((1, tt, D_CONV), lambda i, j: (i, j, 0)),
                  pl.BlockSpec((1, N_HEADS, HEAD_DIM, HEAD_DIM), st(4)),
                  pl.BlockSpec((1, 1, P_RWKV_PAD), st(3)),
                  pl.BlockSpec((1, CONV_HIST, D_CONV), st(3)),
                  row(P_RWKV_PAD),
                  _const_spec((LORA_W + LORA_A, 2 * D_RWKV)), row(2 * D_RWKV),
                  _const_spec((LORA_G_PAD, D_RWKV)),
                  row(D_RWKV), row(D_RWKV), row(D_RWKV), row(D_RWKV), row(D_RWKV),
                  _const_spec((CONV_HIST, D_CONV)), row(D_CONV), row(D_CONV), row(D_CONV)],
        out_specs=[pl.BlockSpec((1, tt, D_MODEL), lambda i, j: (i, j, 0)),
                   pl.BlockSpec((1, N_HEADS, HEAD_DIM, HEAD_DIM), lambda i, j: (i, 0, 0, 0)),
                   pl.BlockSpec((1, 1, P_RWKV_PAD), lambda i, j: (i, 0, 0)),
                   pl.BlockSpec((1, CONV_HIST, D_CONV), lambda i, j: (i, 0, 0))],
        out_shape=[jax.ShapeDtypeStruct((b, t, D_MODEL), F32),
                   jax.ShapeDtypeStruct((b, N_HEADS, HEAD_DIM, HEAD_DIM), F32),
                   jax.ShapeDtypeStruct((b, 1, P_RWKV_PAD), F32),
                   jax.ShapeDtypeStruct((b, CONV_HIST, D_CONV), F32)],
        scratch_shapes=[pltpu.VMEM((N_HEADS, HEAD_DIM, HEAD_DIM), F32),
                        pltpu.VMEM((8, P_RWKV_PAD), F32),
                        pltpu.VMEM((CONV_HIST + tt, D_CONV), F32),
                        pltpu.VMEM((tt, D_RWKV), BF16), pltpu.VMEM((tt, D_RWKV), BF16),
                        pltpu.VMEM((tt, D_RWKV), BF16), pltpu.VMEM((tt, D_RWKV), BF16),
                        pltpu.VMEM((tt, D_RWKV), BF16),
                        pltpu.VMEM((8 * (tt // chunk), D_RWKV), F32),
                        pltpu.VMEM((tt, D_RWKV), F32)],
        compiler_params=pltpu.CompilerParams(dimension_semantics=("parallel", "arbitrary"),
                                             vmem_limit_bytes=VMEM_LIMIT),
        name="mixer",
    )(pr, u, wkv0, shift0, conv0, lp["mu"], lp["wwa"], lp["w0a0"], lp["lg"], lp["k_k"], lp["k_a"], lp["r_k"],
      lp["gn_g"], lp["gn_b"], lp["conv_w"], lp["conv_b"], lp["cln_g"], lp["cln_b"])


FF_BLOCK = 1024


def _ffn_kernel(h_ref, mix_ref, wo_ref, g_ref, wu_ref, wd_ref, gf_ref, out_ref, *, final):
    h1 = h_ref[...] + _dot(mix_ref[...].astype(BF16), wo_ref[...])
    hn = _rms(h1, g_ref[...]).astype(BF16)
    acc = h1
    for c in range(D_FF // FF_BLOCK):
        cols = slice(c * FF_BLOCK, (c + 1) * FF_BLOCK)
        up = jnp.maximum(_dot(hn, wu_ref[:, cols]), 0.0)
        acc = acc + _dot((up * up).astype(BF16), wd_ref[cols, :])
    out_ref[...] = _rms(acc, gf_ref[...]) if final else acc


def _ffn(h, mix, lp, norm_final, tm, final):
    n = h.shape[0]
    tile = pl.BlockSpec((tm, D_MODEL), lambda i: (i, 0))
    return pl.pallas_call(
        functools.partial(_ffn_kernel, final=final),
        grid=(n // tm,),
        in_specs=[tile, tile,
                  _const_spec((D_MODEL, D_MODEL)), _const_spec((1, D_MODEL)),
                  _const_spec((D_MODEL, D_FF)), _const_spec((D_FF, D_MODEL)), _const_spec((1, D_MODEL))],
        out_specs=tile,
        out_shape=jax.ShapeDtypeStruct((n, D_MODEL), F32),
        compiler_params=pltpu.CompilerParams(dimension_semantics=("parallel",),
                                             vmem_limit_bytes=VMEM_LIMIT),
        name="ffn",
    )(h, mix, lp["w_out"], lp["norm_ffn"], lp["w_up"], lp["w_down"], norm_final)


def _layer_params(l, norm_mix, w_in, mu_shift, w0, lora_w, a0, lora_a, lora_g, k_k, k_a, r_k, gn_g, gn_b,
                  conv_w, conv_b, cln_g, cln_b, w_out, norm_ffn, w_up, w_down):
    row = lambda x: x.reshape(1, -1).astype(F32)
    pad_cols = lambda x, n: jnp.pad(x, ((0, 0), (0, n - x.shape[1])))
    wl = w_in[l]
    w_in_p = jnp.concatenate([pad_cols(wl[:, :P_RWKV], P_RWKV_PAD), wl[:, P_RWKV:]], axis=1).astype(BF16)
    zeros = jnp.zeros((LORA_W, D_RWKV), F32)
    wwa = jnp.concatenate([jnp.concatenate([lora_w[l], zeros], axis=1),
                           jnp.concatenate([zeros, lora_a[l]], axis=1)], axis=0).astype(BF16)
    return dict(
        norm_mix=row(norm_mix[l]), w_in=w_in_p,
        mu=pad_cols(row(mu_shift[l]), P_RWKV_PAD),
        wwa=wwa, w0a0=jnp.concatenate([row(w0[l]), row(a0[l])], axis=1),
        lg=jnp.pad(lora_g[l], ((0, LORA_G_PAD - LORA_G), (0, 0))).astype(BF16),
        k_k=row(k_k[l]), k_a=row(k_a[l]), r_k=row(r_k[l]), gn_g=row(gn_g[l]), gn_b=row(gn_b[l]),
        conv_w=jnp.pad(conv_w[l], ((0, CONV_HIST - CONV_WIDTH), (0, 0))),
        conv_b=row(conv_b[l]), cln_g=row(cln_g[l]), cln_b=row(cln_b[l]),
        w_out=w_out[l].astype(BF16), norm_ffn=row(norm_ffn[l]),
        w_up=w_up[l].astype(BF16), w_down=w_down[l].astype(BF16))


def _run_group(x, wkv0, shift0, conv0, params, norm_final, tm, tt, chunk):
    b, t, _ = x.shape
    h = x.reshape(b * t, D_MODEL)
    wkv, shift, conv = [], [], []
    for l, lp in enumerate(params):
        pr, u = _proj(h, lp["norm_mix"], lp["w_in"], tm)
        mix, s1, s2, s3 = _mixer(pr.reshape(b, t, P_RWKV_PAD), u.reshape(b, t, D_CONV),
                                 wkv0[l], shift0[l], conv0[l], lp, tt, chunk)
        h = _ffn(h, mix.reshape(b * t, D_MODEL), lp, norm_final, tm, final=(l == len(params) - 1))
        wkv.append(s1)
        shift.append(s2)
        conv.append(s3)
    return h.reshape(b, t, D_MODEL), wkv, shift, conv


def kernel(x_prompt, x_sample, state_wkv, state_shift, cache_conv, meta_tokens, norm_mix, w_in, mu_shift, w0, lora_w, a0, lora_a, lora_g, k_k, k_a, r_k, gn_g, gn_b, conv_w, conv_b, cln_g, cln_b, w_out, norm_ffn, w_up, w_down, norm_final):
    weights = (norm_mix, w_in, mu_shift, w0, lora_w, a0, lora_a, lora_g, k_k, k_a, r_k, gn_g, gn_b,
               conv_w, conv_b, cln_g, cln_b, w_out, norm_ffn, w_up, w_down)
    params = [_layer_params(l, *weights) for l in range(DEPTH)]
    nf = norm_final.reshape(1, D_MODEL)
    db, dt, _ = x_sample.shape
    bp, tp, _ = x_prompt.shape
    hist_pad = CONV_HIST - (CONV_WIDTH - 1)

    xs = jnp.concatenate([x_sample, meta_tokens[None].astype(x_sample.dtype)], axis=0)
    nb = db + 1
    wkv0 = jnp.pad(state_wkv, ((0, 0), (0, 1), (0, 0), (0, 0), (0, 0)))
    shift0 = jnp.pad(state_shift, ((0, 0), (0, 1), (0, P_RWKV_PAD - P_RWKV)))[:, :, None, :]
    conv0 = jnp.pad(cache_conv, ((0, 0), (0, 1), (hist_pad, 0), (0, 0)))
    ys, wkv_s, shift_s, conv_s = _run_group(xs, list(wkv0), list(shift0), list(conv0), params, nf,
                                            tm=nb * dt, tt=dt, chunk=dt)
    yp, wkv_p, shift_p, conv_p = _run_group(
        x_prompt, [s[db:] for s in wkv_s], [s[db:] for s in shift_s], [s[db:] for s in conv_s],
        params, nf, tm=512, tt=256, chunk=WKV_CHUNK)

    stack = lambda xs_, f: jnp.stack([f(x) for x in xs_])
    return (yp, ys[:db],
            stack(wkv_p, lambda s: s),
            stack(shift_p, lambda s: s[:, 0, :P_RWKV]),
            stack(conv_p, lambda s: s[:, hist_pad:]),
            stack(wkv_s, lambda s: s[:db]),
            stack(shift_s, lambda s: s[:db, 0, :P_RWKV]),
            stack(conv_s, lambda s: s[:db, hist_pad:]))
```

```python
import functools

import jax
import jax.numpy as jnp
from jax import lax
from jax.experimental import pallas as pl
from jax.experimental.pallas import tpu as pltpu

D_MODEL = 1024
DEPTH = 4
N_META = 16
HEAD_DIM = 64
N_HEADS = 8
D_RWKV = N_HEADS * HEAD_DIM
D_CONV = D_MODEL - D_RWKV
LORA_W = 64
LORA_A = 64
LORA_G = 160
CONV_WIDTH = 31
D_FF = 4 * D_MODEL
P_RWKV = 3 * D_RWKV + LORA_W + LORA_A + LORA_G
NORM_EPS = 1e-6
LN_EPS = 1e-5
GN_EPS = 64e-5
DECAY_SCALE = 0.606531

LANES = 128
P_RWKV_PAD = 15 * LANES
LORA_G_PAD = P_RWKV_PAD - (3 * D_RWKV + LORA_W + LORA_A)
O_WA = 3 * D_RWKV
O_G = O_WA + LORA_W + LORA_A
P_IN_PAD = P_RWKV_PAD + 2 * D_CONV
CONV_HIST = 32
WKV_CHUNK = 64
VMEM_LIMIT = 48 * 1024 * 1024

F32 = jnp.float32
BF16 = jnp.bfloat16


def _dot(a, b):
    return jnp.dot(a, b, preferred_element_type=F32)


def _dot_nt(a, b):
    return lax.dot_general(a, b, (((1,), (1,)), ((), ())), preferred_element_type=F32)


def _dot_tn(a, b):
    return lax.dot_general(a, b, (((0,), (0,)), ((), ())), preferred_element_type=F32)


def _split(x):
    hi = x.astype(BF16)
    lo = (x - hi.astype(F32)).astype(BF16)
    return hi, lo


def _dot3(a, b):
    ah, al = _split(a)
    bh, bl = _split(b)
    return _dot(ah, bh) + (_dot(al, bh) + _dot(ah, bl))


def _rms(x, g):
    return x * lax.rsqrt(jnp.mean(x * x, axis=-1, keepdims=True) + NORM_EPS) * g


def _const_spec(shape):
    return pl.BlockSpec(shape, lambda *_: (0,) * len(shape))


def _proj_kernel(h_ref, g_ref, w_ref, pr_ref, u_ref):
    xn = _rms(h_ref[...], g_ref[...]).astype(BF16)
    pr_ref[...] = _dot(xn, w_ref[:, :P_RWKV_PAD])
    ca = _dot(xn, w_ref[:, P_RWKV_PAD:P_RWKV_PAD + D_CONV])
    cb = _dot(xn, w_ref[:, P_RWKV_PAD + D_CONV:])
    u_ref[...] = ca * jax.nn.sigmoid(cb)


def _proj(h, g, w, tm):
    n = h.shape[0]
    return pl.pallas_call(
        _proj_kernel,
        grid=(n // tm,),
        in_specs=[pl.BlockSpec((tm, D_MODEL), lambda i: (i, 0)),
                  _const_spec((1, D_MODEL)),
                  _const_spec((D_MODEL, P_IN_PAD))],
        out_specs=[pl.BlockSpec((tm, P_RWKV_PAD), lambda i: (i, 0)),
                   pl.BlockSpec((tm, D_CONV), lambda i: (i, 0))],
        out_shape=[jax.ShapeDtypeStruct((n, P_RWKV_PAD), F32),
                   jax.ShapeDtypeStruct((n, D_CONV), F32)],
        compiler_params=pltpu.CompilerParams(dimension_semantics=("parallel",),
                                             vmem_limit_bytes=VMEM_LIMIT),
        name="proj",
    )(h, g, w)


def _head_sum(x):
    outs = []
    for j in range(D_RWKV // LANES):
        xj = x[:, j * LANES:(j + 1) * LANES]
        low = lax.broadcasted_iota(jnp.int32, xj.shape, 1) < HEAD_DIM
        s_lo = jnp.sum(jnp.where(low, xj, 0.0), axis=-1, keepdims=True)
        s_hi = jnp.sum(jnp.where(low, 0.0, xj), axis=-1, keepdims=True)
        outs.append(jnp.where(low, s_lo, s_hi))
    return jnp.concatenate(outs, axis=1)


def _wkv_chunk_heads(s0, kk, rt, bt, kt, v, wc, masks):
    strict, incl, eye, offs = masks
    heads = range(len(s0))
    c = kk[0].shape[0]
    lhs = [jnp.concatenate([kk[h], rt[h]], axis=0) for h in heads]
    gb = [_dot_nt(lhs[h], bt[h]) for h in heads]
    gk = [_dot_nt(lhs[h], kt[h]) for h in heads]
    x = [_dot_nt(lhs[h], s0[h].astype(BF16)) for h in heads]
    lb = [jnp.where(strict, gb[h][:c], 0.0) for h in heads]
    lkv = [_dot(jnp.where(strict, gk[h][:c], 0.0).astype(BF16), v[h]) for h in heads]
    t = [eye - jnp.where(offs[0], lb[h], 0.0) for h in heads]
    for off in offs[1:]:
        tb = [t[h].astype(BF16) for h in heads]
        lt = [_dot(jnp.where(off, lb[h], 0.0).astype(BF16), tb[h]).astype(BF16) for h in heads]
        t = [t[h] - _dot(tb[h], lt[h]) for h in heads]
    u = [_dot3(t[h], -x[h][:c] - lkv[h]).astype(BF16) for h in heads]
    o = [x[h][c:] + _dot(jnp.where(incl, gb[h][c:], 0.0).astype(BF16), u[h])
         + _dot(jnp.where(incl, gk[h][c:], 0.0).astype(BF16), v[h]) for h in heads]
    s1 = [(s0[h] + _dot_tn(jnp.concatenate([u[h], v[h]], axis=0), jnp.concatenate([bt[h], kt[h]], axis=0))) * wc[h]
          for h in heads]
    return o, s1


def _chunk_masks(c):
    ti = lax.broadcasted_iota(jnp.int32, (c, c), 0)
    si = lax.broadcasted_iota(jnp.int32, (c, c), 1)
    offs = []
    n = 1
    while n < c:
        same = ((ti ^ si) & ~(2 * n - 1)) == 0
        offs.append(same & ((ti & n) != 0) & ((si & n) == 0))
        n *= 2
    return ti > si, ti >= si, jnp.where(ti == si, 1.0, 0.0).astype(F32), offs


def _mixer_kernel(pr_ref, u_ref, wkv0_ref, shift0_ref, conv0_ref,
                  mu_ref, wwa_ref, w0a0_ref, lg_ref, kk_ref, ka_ref, rk_ref, gng_ref, gnb_ref,
                  cw_ref, cb_ref, clg_ref, clb_ref,
                  mix_ref, wkv_ref, shift_ref, conv_ref,
                  s_ref, carry_ref, ext_ref, kk_s, rt_s, bt_s, kt_s, v_s, wc_s, o_s, *, chunk):
    t_idx = pl.program_id(1)
    tt = pr_ref.shape[1]
    n_chunks = tt // chunk

    @pl.when(t_idx == 0)
    def _():
        s_ref[...] = wkv0_ref[0]
        carry_ref[0:1, :] = shift0_ref[0]
        ext_ref[0:CONV_HIST, :] = conv0_ref[0]

    p = pr_ref[0]
    row = lax.broadcasted_iota(jnp.int32, p.shape, 0)
    prev = jnp.where(row == 0, carry_ref[0:1, :], pltpu.roll(p, 1, 0))
    xs = p + (prev - p) * mu_ref[...]
    carry_ref[0:1, :] = p[tt - 1:tt, :]

    r = xs[:, 0:D_RWKV]
    k = xs[:, D_RWKV:2 * D_RWKV]
    v = xs[:, 2 * D_RWKV:3 * D_RWKV]
    xwa = xs[:, O_WA:O_G]
    lane = lax.broadcasted_iota(jnp.int32, xwa.shape, 1)
    wa = _dot(jnp.where(lane < LORA_W, jnp.tanh(xwa), xwa).astype(BF16), wwa_ref[...]) + w0a0_ref[...]
    logw = -DECAY_SCALE * jax.nn.sigmoid(wa[:, :D_RWKV])
    a = jax.nn.sigmoid(wa[:, D_RWKV:])
    g = _dot(jax.nn.sigmoid(xs[:, O_G:]).astype(BF16), lg_ref[...])

    kk = k * kk_ref[...]
    kk = kk / jnp.maximum(jnp.sqrt(_head_sum(kk * kk)), 1e-12)
    k = k * (1.0 + (a - 1.0) * ka_ref[...])
    bonus = _head_sum(r * k * rk_ref[...]) * v

    ti = lax.broadcasted_iota(jnp.int32, (tt, tt), 0)
    si = lax.broadcasted_iota(jnp.int32, (tt, tt), 1)
    tril = jnp.where((ti >= si) & ((ti ^ si) < chunk), 1.0, 0.0).astype(BF16)
    l_hi, l_lo = _split(logw)
    l_lo2 = (logw - l_hi.astype(F32) - l_lo.astype(F32)).astype(BF16)
    cl = _dot(tril, l_hi) + (_dot(tril, l_lo) + _dot(tril, l_lo2))
    w_inc = jnp.exp(cl)
    w_inv = jnp.exp(-cl)
    kk_s[...] = (kk * jnp.exp(cl - logw)).astype(BF16)
    rt_s[...] = (r * w_inc).astype(BF16)
    bt_s[...] = (kk * a * w_inv).astype(BF16)
    kt_s[...] = (k * w_inv).astype(BF16)
    v_s[...] = v.astype(BF16)
    for c in range(n_chunks):
        wc_s[c * 8:(c + 1) * 8, :] = jnp.broadcast_to(w_inc[(c + 1) * chunk - 1:(c + 1) * chunk, :], (8, D_RWKV))

    masks = _chunk_masks(chunk)

    def chunk_body(c, carry):
        rows = pl.ds(pl.multiple_of(c * chunk, chunk), chunk)
        wc_rows = pl.ds(pl.multiple_of(c * 8, 8), 8)
        cols = [slice(h * HEAD_DIM, (h + 1) * HEAD_DIM) for h in range(N_HEADS)]
        ld = lambda ref: [ref[rows, cs] for cs in cols]
        o, s1 = _wkv_chunk_heads([s_ref[h] for h in range(N_HEADS)], ld(kk_s), ld(rt_s), ld(bt_s), ld(kt_s), ld(v_s),
                                 [wc_s[wc_rows, cs][0:1] for cs in cols], masks)
        for h in range(N_HEADS):
            o_s[rows, cols[h]] = o[h]
            s_ref[h] = s1[h]
        return carry

    lax.fori_loop(0, n_chunks, chunk_body, 0)

    o = o_s[...]
    mean = _head_sum(o) * (1.0 / HEAD_DIM)
    oc = o - mean
    var = _head_sum(oc * oc) * (1.0 / HEAD_DIM)
    y_r = (oc * lax.rsqrt(var + GN_EPS) * gng_ref[...] + gnb_ref[...] + bonus) * g
    mix_ref[0, :, 0:D_RWKV] = y_r

    ext_ref[CONV_HIST:CONV_HIST + tt, :] = u_ref[0]
    acc = jnp.zeros((tt, D_CONV), F32) + cb_ref[...]
    for j in range(CONV_WIDTH):
        off = CONV_HIST - (CONV_WIDTH - 1) + j
        acc = acc + ext_ref[off:off + tt, :] * cw_ref[j:j + 1, :]
    m = jnp.mean(acc, axis=-1, keepdims=True)
    cc = acc - m
    cv = jnp.mean(cc * cc, axis=-1, keepdims=True)
    ln = cc * lax.rsqrt(cv + LN_EPS) * clg_ref[...] + clb_ref[...]
    mix_ref[0, :, D_RWKV:D_MODEL] = ln * jax.nn.sigmoid(ln)
    hist = ext_ref[tt:tt + CONV_HIST, :]
    ext_ref[0:CONV_HIST, :] = hist

    @pl.when(t_idx == pl.num_programs(1) - 1)
    def _():
        wkv_ref[0] = s_ref[...]
        shift_ref[0] = p[tt - 1:tt, :]
        conv_ref[0] = hist


def _mixer(pr, u, wkv0, shift0, conv0, lp, tt, chunk):
    b, t, _ = pr.shape
    per = wkv0.shape[0] == b

    def st(nd):
        return (lambda i, j: (i,) + (0,) * (nd - 1)) if per else (lambda i, j: (0,) * nd)

    row = lambda n: _const_spec((1, n))
    kern = functools.partial(_mixer_kernel, chunk=chunk)
    return pl.pallas_call(
        kern,
        grid=(b, t // tt),
        in_specs=[pl.BlockSpec((1, tt, P_RWKV_PAD), lambda i, j: (i, j, 0)),
                  pl.BlockSpec((1, tt, D_CONV), lambda i, j: (i, j, 0)),
                  pl.BlockSpec((1, N_HEADS, HEAD_DIM, HEAD_DIM), st(4)),
                  pl.BlockSpec((1, 1, P_RWKV_PAD), st(3)),
                  pl.BlockSpec((1, CONV_HIST, D_CONV), st(3)),
                  row(P_RWKV_PAD),
                  _const_spec((LORA_W + LORA_A, 2 * D_RWKV)), row(2 * D_RWKV),
                  _const_spec((LORA_G_PAD, D_RWKV)),
                  row(D_RWKV), row(D_RWKV), row(D_RWKV), row(D_RWKV), row(D_RWKV),
                  _const_spec((CONV_HIST, D_CONV)), row(D_CONV), row(D_CONV), row(D_CONV)],
        out_specs=[pl.BlockSpec((1, tt, D_MODEL), lambda i, j: (i, j, 0)),
                   pl.BlockSpec((1, N_HEADS, HEAD_DIM, HEAD_DIM), lambda i, j: (i, 0, 0, 0)),
                   pl.BlockSpec((1, 1, P_RWKV_PAD), lambda i, j: (i, 0, 0)),
                   pl.BlockSpec((1, CONV_HIST, D_CONV), lambda i, j: (i, 0, 0))],
        out_shape=[jax.ShapeDtypeStruct((b, t, D_MODEL), F32),
                   jax.ShapeDtypeStruct((b, N_HEADS, HEAD_DIM, HEAD_DIM), F32),
                   jax.ShapeDtypeStruct((b, 1, P_RWKV_PAD), F32),
                   jax.ShapeDtypeStruct((b, CONV_HIST, D_CONV), F32)],
        scratch_shapes=[pltpu.VMEM((N_HEADS, HEAD_DIM, HEAD_DIM), F32),
                        pltpu.VMEM((8, P_RWKV_PAD), F32),
                        pltpu.VMEM((CONV_HIST + tt, D_CONV), F32),
                        pltpu.VMEM((tt, D_RWKV), BF16), pltpu.VMEM((tt, D_RWKV), BF16),
                        pltpu.VMEM((tt, D_RWKV), BF16), pltpu.VMEM((tt, D_RWKV), BF16),
                        pltpu.VMEM((tt, D_RWKV), BF16),
                        pltpu.VMEM((8 * (tt // chunk), D_RWKV), F32),
                        pltpu.VMEM((tt, D_RWKV), F32)],
        compiler_params=pltpu.CompilerParams(dimension_semantics=("parallel", "arbitrary"),
                                             vmem_limit_bytes=VMEM_LIMIT),
        name="mixer",
    )(pr, u, wkv0, shift0, conv0, lp["mu"], lp["wwa"], lp["w0a0"], lp["lg"], lp["k_k"], lp["k_a"], lp["r_k"],
      lp["gn_g"], lp["gn_b"], lp["conv_w"], lp["conv_b"], lp["cln_g"], lp["cln_b"])


FF_BLOCK = 1024


def _ffn_kernel(h_ref, mix_ref, wo_ref, g_ref, wu_ref, wd_ref, gf_ref, out_ref, *, final):
    h1 = h_ref[...] + _dot(mix_ref[...].astype(BF16), wo_ref[...])
    hn = _rms(h1, g_ref[...]).astype(BF16)
    acc = h1
    for c in range(D_FF // FF_BLOCK):
        cols = slice(c * FF_BLOCK, (c + 1) * FF_BLOCK)
        up = jnp.maximum(_dot(hn, wu_ref[:, cols]), 0.0)
        acc = acc + _dot((up * up).astype(BF16), wd_ref[cols, :])
    out_ref[...] = _rms(acc, gf_ref[...]) if final else acc


def _ffn(h, mix, lp, norm_final, tm, final):
    n = h.shape[0]
    tile = pl.BlockSpec((tm, D_MODEL), lambda i: (i, 0))
    return pl.pallas_call(
        functools.partial(_ffn_kernel, final=final),
        grid=(n // tm,),
        in_specs=[tile, tile,
                  _const_spec((D_MODEL, D_MODEL)), _const_spec((1, D_MODEL)),
                  _const_spec((D_MODEL, D_FF)), _const_spec((D_FF, D_MODEL)), _const_spec((1, D_MODEL))],
        out_specs=tile,
        out_shape=jax.ShapeDtypeStruct((n, D_MODEL), F32),
        compiler_params=pltpu.CompilerParams(dimension_semantics=("parallel",),
                                             vmem_limit_bytes=VMEM_LIMIT),
        name="ffn",
    )(h, mix, lp["w_out"], lp["norm_ffn"], lp["w_up"], lp["w_down"], norm_final)


def _layer_params(l, norm_mix, w_in, mu_shift, w0, lora_w, a0, lora_a, lora_g, k_k, k_a, r_k, gn_g, gn_b,
                  conv_w, conv_b, cln_g, cln_b, w_out, norm_ffn, w_up, w_down):
    row = lambda x: x.reshape(1, -1).astype(F32)
    pad_cols = lambda x, n: jnp.pad(x, ((0, 0), (0, n - x.shape[1])))
    wl = w_in[l]
    w_in_p = jnp.concatenate([pad_cols(wl[:, :P_RWKV], P_RWKV_PAD), wl[:, P_RWKV:]], axis=1).astype(BF16)
    zeros = jnp.zeros((LORA_W, D_RWKV), F32)
    wwa = jnp.concatenate([jnp.concatenate([lora_w[l], zeros], axis=1),
                           jnp.concatenate([zeros, lora_a[l]], axis=1)], axis=0).astype(BF16)
    return dict(
        norm_mix=row(norm_mix[l]), w_in=w_in_p,
        mu=pad_cols(row(mu_shift[l]), P_RWKV_PAD),
        wwa=wwa, w0a0=jnp.concatenate([row(w0[l]), row(a0[l])], axis=1),
        lg=jnp.pad(lora_g[l], ((0, LORA_G_PAD - LORA_G), (0, 0))).astype(BF16),
        k_k=row(k_k[l]), k_a=row(k_a[l]), r_k=row(r_k[l]), gn_g=row(gn_g[l]), gn_b=row(gn_b[l]),
        conv_w=jnp.pad(conv_w[l], ((0, CONV_HIST - CONV_WIDTH), (0, 0))),
        conv_b=row(conv_b[l]), cln_g=row(cln_g[l]), cln_b=row(cln_b[l]),
        w_out=w_out[l].astype(BF16), norm_ffn=row(norm_ffn[l]),
        w_up=w_up[l].astype(BF16), w_down=w_down[l].astype(BF16))


def _run_group(x, wkv0, shift0, conv0, params, norm_final, tm, tt, chunk):
    b, t, _ = x.shape
    h = x.reshape(b * t, D_MODEL)
    wkv, shift, conv = [], [], []
    for l, lp in enumerate(params):
        pr, u = _proj(h, lp["norm_mix"], lp["w_in"], tm)
        mix, s1, s2, s3 = _mixer(pr.reshape(b, t, P_RWKV_PAD), u.reshape(b, t, D_CONV),
                                 wkv0[l], shift0[l], conv0[l], lp, tt, chunk)
        h = _ffn(h, mix.reshape(b * t, D_MODEL), lp, norm_final, tm, final=(l == len(params) - 1))
        wkv.append(s1)
        shift.append(s2)
        conv.append(s3)
    return h.reshape(b, t, D_MODEL), wkv, shift, conv


def kernel(x_prompt, x_sample, state_wkv, state_shift, cache_conv, meta_tokens, norm_mix, w_in, mu_shift, w0, lora_w, a0, lora_a, lora_g, k_k, k_a, r_k, gn_g, gn_b, conv_w, conv_b, cln_g, cln_b, w_out, norm_ffn, w_up, w_down, norm_final):
    weights = (norm_mix, w_in, mu_shift, w0, lora_w, a0, lora_a, lora_g, k_k, k_a, r_k, gn_g, gn_b,
               conv_w, conv_b, cln_g, cln_b, w_out, norm_ffn, w_up, w_down)
    params = [_layer_params(l, *weights) for l in range(DEPTH)]
    nf = norm_final.reshape(1, D_MODEL)
    db, dt, _ = x_sample.shape
    bp, tp, _ = x_prompt.shape
    hist_pad = CONV_HIST - (CONV_WIDTH - 1)

    xs = jnp.concatenate([x_sample, meta_tokens[None].astype(x_sample.dtype)], axis=0)
    nb = db + 1
    wkv0 = jnp.pad(state_wkv, ((0, 0), (0, 1), (0, 0), (0, 0), (0, 0)))
    shift0 = jnp.pad(state_shift, ((0, 0), (0, 1), (0, P_RWKV_PAD - P_RWKV)))[:, :, None, :]
    conv0 = jnp.pad(cache_conv, ((0, 0), (0, 1), (hist_pad, 0), (0, 0)))
    ys, wkv_s, shift_s, conv_s = _run_group(xs, list(wkv0), list(shift0), list(conv0), params, nf,
                                            tm=nb * dt, tt=dt, chunk=dt)
    yp, wkv_p, shift_p, conv_p = _run_group(
        x_prompt, [s[db:] for s in wkv_s], [s[db:] for s in shift_s], [s[db:] for s in conv_s],
        params, nf, tm=512, tt=256, chunk=WKV_CHUNK)

    stack = lambda xs_, f: jnp.stack([f(x) for x in xs_])
    return (yp, ys[:db],
            stack(wkv_p, lambda s: s),
            stack(shift_p, lambda s: s[:, 0, :P_RWKV]),
            stack(conv_p, lambda s: s[:, hist_pad:]),
            stack(wkv_s, lambda s: s[:db]),
            stack(shift_s, lambda s: s[:db, 0, :P_RWKV]),
            stack(conv_s, lambda s: s[:db, hist_pad:]))
```

```python
import functools

import jax
import jax.numpy as jnp
from jax import lax
from jax.experimental import pallas as pl
from jax.experimental.pallas import tpu as pltpu

D_MODEL = 1024
DEPTH = 4
N_META = 16
HEAD_DIM = 64
N_HEADS = 8
D_RWKV = N_HEADS * HEAD_DIM
D_CONV = D_MODEL - D_RWKV
LORA_W = 64
LORA_A = 64
LORA_G = 160
CONV_WIDTH = 31
D_FF = 4 * D_MODEL
P_RWKV = 3 * D_RWKV + LORA_W + LORA_A + LORA_G
NORM_EPS = 1e-6
LN_EPS = 1e-5
GN_EPS = 64e-5
DECAY_SCALE = 0.606531

LANES = 128
P_RWKV_PAD = 15 * LANES
LORA_G_PAD = P_RWKV_PAD - (3 * D_RWKV + LORA_W + LORA_A)
O_WA = 3 * D_RWKV
O_G = O_WA + LORA_W + LORA_A
P_IN_PAD = P_RWKV_PAD + 2 * D_CONV
CONV_HIST = 32
WKV_CHUNK = 64
PROMPT_STREAMS = 4
PROMPT_TILE = 128
SMALL_STREAMS = 11
VMEM_LIMIT = 48 * 1024 * 1024

F32 = jnp.float32
BF16 = jnp.bfloat16


def _dot(a, b):
    return jnp.dot(a, b, preferred_element_type=F32)


def _dot_nt(a, b):
    return lax.dot_general(a, b, (((1,), (1,)), ((), ())), preferred_element_type=F32)


def _dot_tn(a, b):
    return lax.dot_general(a, b, (((0,), (0,)), ((), ())), preferred_element_type=F32)


def _split(x):
    hi = x.astype(BF16)
    lo = (x - hi.astype(F32)).astype(BF16)
    return hi, lo


def _dot3(a, b):
    ah, al = _split(a)
    bh, bl = _split(b)
    return _dot(ah, bh) + (_dot(al, bh) + _dot(ah, bl))


def _rms(x, g):
    return x * lax.rsqrt(jnp.mean(x * x, axis=-1, keepdims=True) + NORM_EPS) * g


def _const_spec(shape):
    return pl.BlockSpec(shape, lambda *_: (0,) * len(shape))


def _proj_kernel(h_ref, g_ref, w_ref, pr_ref, u_ref):
    xn = _rms(h_ref[...], g_ref[...]).astype(BF16)
    pr_ref[...] = _dot(xn, w_ref[:, :P_RWKV_PAD])
    ca = _dot(xn, w_ref[:, P_RWKV_PAD:P_RWKV_PAD + D_CONV])
    cb = _dot(xn, w_ref[:, P_RWKV_PAD + D_CONV:])
    u_ref[...] = ca * jax.nn.sigmoid(cb)


def _proj(h, g, w, tm):
    n = h.shape[0]
    return pl.pallas_call(
        _proj_kernel,
        grid=(n // tm,),
        in_specs=[pl.BlockSpec((tm, D_MODEL), lambda i: (i, 0)),
                  _const_spec((1, D_MODEL)),
                  _const_spec((D_MODEL, P_IN_PAD))],
        out_specs=[pl.BlockSpec((tm, P_RWKV_PAD), lambda i: (i, 0)),
                   pl.BlockSpec((tm, D_CONV), lambda i: (i, 0))],
        out_shape=[jax.ShapeDtypeStruct((n, P_RWKV_PAD), F32),
                   jax.ShapeDtypeStruct((n, D_CONV), F32)],
        compiler_params=pltpu.CompilerParams(dimension_semantics=("parallel",),
                                             vmem_limit_bytes=VMEM_LIMIT),
        name="proj",
    )(h, g, w)


def _head_sum(x):
    i = lax.broadcasted_iota(jnp.int32, (LANES, LANES), 0)
    j = lax.broadcasted_iota(jnp.int32, (LANES, LANES), 1)
    ones = jnp.where(((i ^ j) & HEAD_DIM) == 0, 1.0, 0.0).astype(BF16)
    outs = []
    for g in range(D_RWKV // LANES):
        hi, lo = _split(x[:, g * LANES:(g + 1) * LANES])
        outs.append(_dot(hi, ones) + _dot(lo, ones))
    return jnp.concatenate(outs, axis=1)


def _bd(x):
    low = lax.broadcasted_iota(jnp.int32, x.shape, 1) < x.shape[1] // 2
    return jnp.concatenate([jnp.where(low, x, 0.0), jnp.where(low, 0.0, x)], axis=0)


def _pair_masks(c):
    ti = lax.broadcasted_iota(jnp.int32, (c, 2 * c), 0)
    si = lax.broadcasted_iota(jnp.int32, (c, 2 * c), 1) & (c - 1)
    offs = []
    n = 1
    while n < c:
        same = ((ti ^ si) & ~(2 * n - 1)) == 0
        offs.append(same & ((ti & n) != 0) & ((si & n) == 0))
        n *= 2
    return ti > si, ti >= si, jnp.where(ti == si, 1.0, 0.0).astype(F32), offs


def _wkv_prepare(lhs, bdb, bdk, bdv, masks):
    strict, incl, eye, offs = masks
    n = range(len(lhs))
    c = lhs[0].shape[0] // 2
    gb = [_dot_nt(lhs[i], bdb[i]) for i in n]
    gk = [_dot_nt(lhs[i], bdk[i]) for i in n]
    lb = [jnp.where(strict, gb[i][:c], 0.0) for i in n]
    lkv = [_dot(jnp.where(strict, gk[i][:c], 0.0).astype(BF16), bdv[i]) for i in n]
    arb = [jnp.where(incl, gb[i][c:], 0.0).astype(BF16) for i in n]
    ark = [jnp.where(incl, gk[i][c:], 0.0).astype(BF16) for i in n]
    t = [eye - jnp.where(offs[0], lb[i], 0.0) for i in n]
    for off in offs[1:]:
        lt = [_dot(jnp.where(off, lb[i], 0.0).astype(BF16), _bd(t[i]).astype(BF16)) for i in n]
        t = [t[i] - _dot(t[i].astype(BF16), _bd(lt[i]).astype(BF16)) for i in n]
    return t, lkv, arb, ark


def _wkv_advance(s0, lhs, t, lkv, arb, ark, bdv, uv_v, uv_rhs, wc, diag):
    n = range(len(s0))
    c = lhs[0].shape[0] // 2
    x = [_dot_nt(lhs[i], s0[i].astype(BF16)) for i in n]
    u = [_dot3(t[i], _bd(-x[i][:c] - lkv[i])) for i in n]
    o = [x[i][c:] + _dot(arb[i], _bd(u[i]).astype(BF16)) + _dot(ark[i], bdv[i]) for i in n]
    upd = [_dot_tn(jnp.concatenate([u[i].astype(BF16), uv_v[i]], axis=0), uv_rhs[i]) for i in n]
    s1 = [(s0[i] + jnp.where(diag, upd[i], 0.0)) * wc[i] for i in n]
    return o, s1


def _mixer_kernel(pr_ref, u_ref, wkv0_ref, shift0_ref, conv0_ref,
                  mu_ref, wwa_ref, w0a0_ref, lg_ref, kk_ref, ka_ref, rk_ref, gng_ref, gnb_ref,
                  cw_ref, cb_ref, clg_ref, clb_ref,
                  mix_ref, wkv_ref, shift_ref, conv_ref,
                  s_ref, carry_ref, ext_ref, kk_s, rt_s, b_s, blo_s, bhi_s, k_s, klo_s, khi_s,
                  v_s, vlo_s, vhi_s, wc_s, g_s, bonus_s, o_s, *, chunk, shared_init):
    t_idx = pl.program_id(1)
    nb, tt, _ = pr_ref.shape
    n_chunks = tt // chunk
    n_pairs = D_RWKV // LANES

    def prepare_stream(s, carry):
        init = 0 if shared_init else s

        @pl.when(t_idx == 0)
        def _():
            s_ref[s] = wkv0_ref[init]
            carry_ref[s, 0:1, :] = shift0_ref[init]
            ext_ref[s, 0:CONV_HIST, :] = conv0_ref[init]

        rows = pl.ds(pl.multiple_of(s * tt, tt), tt)
        p = pr_ref[s]
        row = lax.broadcasted_iota(jnp.int32, p.shape, 0)
        prev = jnp.where(row == 0, carry_ref[s, 0:1, :], pltpu.roll(p, 1, 0))
        xs = p + (prev - p) * mu_ref[...]
        carry_ref[s, 0:1, :] = p[tt - 1:tt, :]

        r = xs[:, 0:D_RWKV]
        k = xs[:, D_RWKV:2 * D_RWKV]
        v = xs[:, 2 * D_RWKV:3 * D_RWKV]
        xwa = xs[:, O_WA:O_G]
        lane = lax.broadcasted_iota(jnp.int32, xwa.shape, 1)
        wa = _dot(jnp.where(lane < LORA_W, jnp.tanh(xwa), xwa).astype(BF16), wwa_ref[...]) + w0a0_ref[...]
        logw = -DECAY_SCALE * jax.nn.sigmoid(wa[:, :D_RWKV])
        a = jax.nn.sigmoid(wa[:, D_RWKV:])
        g_s[rows, :] = _dot(jax.nn.sigmoid(xs[:, O_G:]).astype(BF16), lg_ref[...])

        kk = k * kk_ref[...]
        kk = kk / jnp.maximum(jnp.sqrt(_head_sum(kk * kk)), 1e-12)
        k = k * (1.0 + (a - 1.0) * ka_ref[...])
        bonus_s[rows, :] = _head_sum(r * k * rk_ref[...]) * v

        ti = lax.broadcasted_iota(jnp.int32, (tt, tt), 0)
        si = lax.broadcasted_iota(jnp.int32, (tt, tt), 1)
        tril = jnp.where((ti >= si) & ((ti ^ si) < chunk), 1.0, 0.0).astype(BF16)
        l_hi, l_lo = _split(logw)
        l_lo2 = (logw - l_hi.astype(F32) - l_lo.astype(F32)).astype(BF16)
        cl = _dot(tril, l_hi) + (_dot(tril, l_lo) + _dot(tril, l_lo2))
        w_inc = jnp.exp(cl)
        w_inv = jnp.exp(-cl)
        low = (lax.broadcasted_iota(jnp.int32, (tt, D_RWKV), 1) & HEAD_DIM) == 0
        kk_s[rows, :] = (kk * jnp.exp(cl - logw)).astype(BF16)
        rt_s[rows, :] = (r * w_inc).astype(BF16)
        for x, nat, lo, hi in ((kk * a * w_inv, b_s, blo_s, bhi_s), (k * w_inv, k_s, klo_s, khi_s),
                               (v, v_s, vlo_s, vhi_s)):
            nat[rows, :] = x.astype(BF16)
            lo[rows, :] = jnp.where(low, x, 0.0).astype(BF16)
            hi[rows, :] = jnp.where(low, 0.0, x).astype(BF16)
        for c in range(n_chunks):
            wc_rows = pl.ds(pl.multiple_of((s * n_chunks + c) * 8, 8), 8)
            wc_s[wc_rows, :] = jnp.broadcast_to(w_inc[(c + 1) * chunk - 1:(c + 1) * chunk, :], (8, D_RWKV))

        ext_ref[s, CONV_HIST:CONV_HIST + tt, :] = u_ref[s]
        first = CONV_HIST - (CONV_WIDTH - 1)
        acc = jnp.zeros((tt, D_CONV), F32) + cb_ref[...]
        ext = ext_ref[s]
        for q in range(8):
            eq = ext if q == 0 else pltpu.roll(ext, CONV_HIST + tt - q, 0)
            for o in range(first + (q - first) % 8, CONV_HIST + 1, 8):
                acc = acc + eq[o - q:o - q + tt, :] * cw_ref[o - first:o - first + 1, :]
        m = jnp.mean(acc, axis=-1, keepdims=True)
        cc = acc - m
        cv = jnp.mean(cc * cc, axis=-1, keepdims=True)
        ln = cc * lax.rsqrt(cv + LN_EPS) * clg_ref[...] + clb_ref[...]
        mix_ref[s, :, D_RWKV:D_MODEL] = ln * jax.nn.sigmoid(ln)
        hist = ext_ref[s, tt:tt + CONV_HIST, :]
        ext_ref[s, 0:CONV_HIST, :] = hist

        @pl.when(t_idx == pl.num_programs(1) - 1)
        def _():
            shift_ref[s] = p[tt - 1:tt, :]
            conv_ref[s] = hist

        return carry

    lax.fori_loop(0, nb, prepare_stream, 0)

    masks = _pair_masks(chunk)
    di = lax.broadcasted_iota(jnp.int32, (LANES, LANES), 0)
    dj = lax.broadcasted_iota(jnp.int32, (LANES, LANES), 1)
    diag = ((di ^ dj) & HEAD_DIM) == 0
    chains = [(s, p) for s in range(nb) for p in range(n_pairs)]
    cat = lambda a, b: jnp.concatenate([a, b], axis=0)
    prepared = []
    for c in range(n_chunks):
        at = lambda ref, s, p: ref[s * tt + c * chunk:s * tt + (c + 1) * chunk, p * LANES:(p + 1) * LANES]
        lhs = [cat(at(kk_s, s, p), at(rt_s, s, p)) for s, p in chains]
        bdb = [cat(at(blo_s, s, p), at(bhi_s, s, p)) for s, p in chains]
        bdk = [cat(at(klo_s, s, p), at(khi_s, s, p)) for s, p in chains]
        bdv = [cat(at(vlo_s, s, p), at(vhi_s, s, p)) for s, p in chains]
        prepared.append((lhs, bdv) + tuple(_wkv_prepare(lhs, bdb, bdk, bdv, masks)))
    for c in range(n_chunks):
        at = lambda ref, s, p: ref[s * tt + c * chunk:s * tt + (c + 1) * chunk, p * LANES:(p + 1) * LANES]
        lhs, bdv, t, lkv, arb, ark = prepared[c]
        uv_v = [at(v_s, s, p) for s, p in chains]
        uv_rhs = [cat(at(b_s, s, p), at(k_s, s, p)) for s, p in chains]
        wc = [wc_s[(s * n_chunks + c) * 8:(s * n_chunks + c) * 8 + 1, p * LANES:(p + 1) * LANES] for s, p in chains]
        o, s1 = _wkv_advance([s_ref[s, p] for s, p in chains], lhs, t, lkv, arb, ark, bdv, uv_v, uv_rhs, wc, diag)
        for i, (s, p) in enumerate(chains):
            o_s[s * tt + c * chunk:s * tt + (c + 1) * chunk, p * LANES:(p + 1) * LANES] = o[i]
            s_ref[s, p] = s1[i]

    def finish_stream(s, carry):
        rows = pl.ds(pl.multiple_of(s * tt, tt), tt)
        o = o_s[rows, :]
        mean = _head_sum(o) * (1.0 / HEAD_DIM)
        oc = o - mean
        var = _head_sum(oc * oc) * (1.0 / HEAD_DIM)
        mix_ref[s, :, 0:D_RWKV] = (oc * lax.rsqrt(var + GN_EPS) * gng_ref[...] + gnb_ref[...]
                                   + bonus_s[rows, :]) * g_s[rows, :]
        return carry

    lax.fori_loop(0, nb, finish_stream, 0)

    @pl.when(t_idx == pl.num_programs(1) - 1)
    def _():
        wkv_ref[...] = s_ref[...]


def _mixer(pr, u, wkv0, shift0, conv0, lp, nb, tt, chunk):
    b, t, _ = pr.shape
    shared = wkv0.shape[0] != b
    n_pairs = D_RWKV // LANES

    def st(*tail):
        if shared:
            return pl.BlockSpec((1,) + tail, lambda i, j: (0,) * (1 + len(tail)))
        return pl.BlockSpec((nb,) + tail, lambda i, j: (i,) + (0,) * len(tail))

    row = lambda n: _const_spec((1, n))
    rows = nb * tt
    bf = lambda: pltpu.VMEM((rows, D_RWKV), BF16)
    kern = functools.partial(_mixer_kernel, chunk=chunk, shared_init=shared)
    return pl.pallas_call(
        kern,
        grid=(b // nb, t // tt),
        in_specs=[pl.BlockSpec((nb, tt, P_RWKV_PAD), lambda i, j: (i, j, 0)),
                  pl.BlockSpec((nb, tt, D_CONV), lambda i, j: (i, j, 0)),
                  st(n_pairs, LANES, LANES), st(1, P_RWKV_PAD), st(CONV_HIST, D_CONV),
                  row(P_RWKV_PAD),
                  _const_spec((LORA_W + LORA_A, 2 * D_RWKV)), row(2 * D_RWKV),
                  _const_spec((LORA_G_PAD, D_RWKV)),
                  row(D_RWKV), row(D_RWKV), row(D_RWKV), row(D_RWKV), row(D_RWKV),
                  _const_spec((CONV_HIST, D_CONV)), row(D_CONV), row(D_CONV), row(D_CONV)],
        out_specs=[pl.BlockSpec((nb, tt, D_MODEL), lambda i, j: (i, j, 0)),
                   pl.BlockSpec((nb, n_pairs, LANES, LANES), lambda i, j: (i, 0, 0, 0)),
                   pl.BlockSpec((nb, 1, P_RWKV_PAD), lambda i, j: (i, 0, 0)),
                   pl.BlockSpec((nb, CONV_HIST, D_CONV), lambda i, j: (i, 0, 0))],
        out_shape=[jax.ShapeDtypeStruct((b, t, D_MODEL), F32),
                   jax.ShapeDtypeStruct((b, n_pairs, LANES, LANES), F32),
                   jax.ShapeDtypeStruct((b, 1, P_RWKV_PAD), F32),
                   jax.ShapeDtypeStruct((b, CONV_HIST, D_CONV), F32)],
        scratch_shapes=[pltpu.VMEM((nb, n_pairs, LANES, LANES), F32),
                        pltpu.VMEM((nb, 8, P_RWKV_PAD), F32),
                        pltpu.VMEM((nb, CONV_HIST + tt, D_CONV), F32),
                        bf(), bf(), bf(), bf(), bf(), bf(), bf(), bf(), bf(), bf(), bf(),
                        pltpu.VMEM((8 * nb * (tt // chunk), D_RWKV), F32),
                        pltpu.VMEM((rows, D_RWKV), F32), pltpu.VMEM((rows, D_RWKV), F32),
                        pltpu.VMEM((rows, D_RWKV), F32)],
        compiler_params=pltpu.CompilerParams(dimension_semantics=("parallel", "arbitrary"),
                                             vmem_limit_bytes=VMEM_LIMIT),
        name="mixer",
    )(pr, u, wkv0, shift0, conv0, lp["mu"], lp["wwa"], lp["w0a0"], lp["lg"], lp["k_k"], lp["k_a"], lp["r_k"],
      lp["gn_g"], lp["gn_b"], lp["conv_w"], lp["conv_b"], lp["cln_g"], lp["cln_b"])


def _pairs_to_blockdiag(s):
    z = jnp.zeros_like(s[:, 0::2])
    top = jnp.concatenate([s[:, 0::2], z], axis=-1)
    bot = jnp.concatenate([z, s[:, 1::2]], axis=-1)
    return jnp.concatenate([top, bot], axis=-2)


def _blockdiag_to_pairs(s):
    even = s[:, :, :HEAD_DIM, :HEAD_DIM]
    odd = s[:, :, HEAD_DIM:, HEAD_DIM:]
    return jnp.stack([even, odd], axis=2).reshape(s.shape[0], N_HEADS, HEAD_DIM, HEAD_DIM)


FF_BLOCK = 1024


def _ffn_kernel(h_ref, mix_ref, wo_ref, g_ref, wu_ref, wd_ref, gf_ref, out_ref, *, final):
    h1 = h_ref[...] + _dot(mix_ref[...].astype(BF16), wo_ref[...])
    hn = _rms(h1, g_ref[...]).astype(BF16)
    acc = h1
    for c in range(D_FF // FF_BLOCK):
        cols = slice(c * FF_BLOCK, (c + 1) * FF_BLOCK)
        up = jnp.maximum(_dot(hn, wu_ref[:, cols]), 0.0)
        acc = acc + _dot((up * up).astype(BF16), wd_ref[cols, :])
    out_ref[...] = _rms(acc, gf_ref[...]) if final else acc


def _ffn(h, mix, lp, norm_final, tm, final):
    n = h.shape[0]
    tile = pl.BlockSpec((tm, D_MODEL), lambda i: (i, 0))
    return pl.pallas_call(
        functools.partial(_ffn_kernel, final=final),
        grid=(n // tm,),
        in_specs=[tile, tile,
                  _const_spec((D_MODEL, D_MODEL)), _const_spec((1, D_MODEL)),
                  _const_spec((D_MODEL, D_FF)), _const_spec((D_FF, D_MODEL)), _const_spec((1, D_MODEL))],
        out_specs=tile,
        out_shape=jax.ShapeDtypeStruct((n, D_MODEL), F32),
        compiler_params=pltpu.CompilerParams(dimension_semantics=("parallel",),
                                             vmem_limit_bytes=VMEM_LIMIT),
        name="ffn",
    )(h, mix, lp["w_out"], lp["norm_ffn"], lp["w_up"], lp["w_down"], norm_final)


def _layer_params(l, norm_mix, w_in, mu_shift, w0, lora_w, a0, lora_a, lora_g, k_k, k_a, r_k, gn_g, gn_b,
                  conv_w, conv_b, cln_g, cln_b, w_out, norm_ffn, w_up, w_down):
    row = lambda x: x.reshape(1, -1).astype(F32)
    pad_cols = lambda x, n: jnp.pad(x, ((0, 0), (0, n - x.shape[1])))
    wl = w_in[l]
    w_in_p = jnp.concatenate([pad_cols(wl[:, :P_RWKV], P_RWKV_PAD), wl[:, P_RWKV:]], axis=1).astype(BF16)
    zeros = jnp.zeros((LORA_W, D_RWKV), F32)
    wwa = jnp.concatenate([jnp.concatenate([lora_w[l], zeros], axis=1),
                           jnp.concatenate([zeros, lora_a[l]], axis=1)], axis=0).astype(BF16)
    return dict(
        norm_mix=row(norm_mix[l]), w_in=w_in_p,
        mu=pad_cols(row(mu_shift[l]), P_RWKV_PAD),
        wwa=wwa, w0a0=jnp.concatenate([row(w0[l]), row(a0[l])], axis=1),
        lg=jnp.pad(lora_g[l], ((0, LORA_G_PAD - LORA_G), (0, 0))).astype(BF16),
        k_k=row(k_k[l]), k_a=row(k_a[l]), r_k=row(r_k[l]), gn_g=row(gn_g[l]), gn_b=row(gn_b[l]),
        conv_w=jnp.pad(conv_w[l], ((0, CONV_HIST - CONV_WIDTH), (0, 0))),
        conv_b=row(conv_b[l]), cln_g=row(cln_g[l]), cln_b=row(cln_b[l]),
        w_out=w_out[l].astype(BF16), norm_ffn=row(norm_ffn[l]),
        w_up=w_up[l].astype(BF16), w_down=w_down[l].astype(BF16))


def _run_group(x, wkv0, shift0, conv0, params, norm_final, tm, nb, tt, chunk):
    b, t, _ = x.shape
    h = x.reshape(b * t, D_MODEL)
    wkv, shift, conv = [], [], []
    for l, lp in enumerate(params):
        pr, u = _proj(h, lp["norm_mix"], lp["w_in"], tm)
        mix, s1, s2, s3 = _mixer(pr.reshape(b, t, P_RWKV_PAD), u.reshape(b, t, D_CONV),
                                 wkv0[l], shift0[l], conv0[l], lp, nb, tt, chunk)
        h = _ffn(h, mix.reshape(b * t, D_MODEL), lp, norm_final, tm, final=(l == len(params) - 1))
        wkv.append(s1)
        shift.append(s2)
        conv.append(s3)
    return h.reshape(b, t, D_MODEL), wkv, shift, conv


def kernel(x_prompt, x_sample, state_wkv, state_shift, cache_conv, meta_tokens, norm_mix, w_in, mu_shift, w0, lora_w, a0, lora_a, lora_g, k_k, k_a, r_k, gn_g, gn_b, conv_w, conv_b, cln_g, cln_b, w_out, norm_ffn, w_up, w_down, norm_final):
    weights = (norm_mix, w_in, mu_shift, w0, lora_w, a0, lora_a, lora_g, k_k, k_a, r_k, gn_g, gn_b,
               conv_w, conv_b, cln_g, cln_b, w_out, norm_ffn, w_up, w_down)
    params = [_layer_params(l, *weights) for l in range(DEPTH)]
    nf = norm_final.reshape(1, D_MODEL)
    db, dt, _ = x_sample.shape
    bp, tp, _ = x_prompt.shape
    hist_pad = CONV_HIST - (CONV_WIDTH - 1)

    xs = jnp.concatenate([x_sample, meta_tokens[None].astype(x_sample.dtype)], axis=0)
    nb = db + 1
    wkv0 = [_pairs_to_blockdiag(jnp.pad(w, ((0, 1), (0, 0), (0, 0), (0, 0)))) for w in state_wkv]
    shift0 = jnp.pad(state_shift, ((0, 0), (0, 1), (0, P_RWKV_PAD - P_RWKV)))[:, :, None, :]
    conv0 = jnp.pad(cache_conv, ((0, 0), (0, 1), (hist_pad, 0), (0, 0)))
    ys, wkv_s, shift_s, conv_s = _run_group(xs, wkv0, list(shift0), list(conv0), params, nf,
                                            tm=nb * dt, nb=SMALL_STREAMS, tt=dt, chunk=dt)
    yp, wkv_p, shift_p, conv_p = _run_group(
        x_prompt, [s[db:] for s in wkv_s], [s[db:] for s in shift_s], [s[db:] for s in conv_s],
        params, nf, tm=512, nb=PROMPT_STREAMS, tt=PROMPT_TILE, chunk=WKV_CHUNK)

    stack = lambda xs_, f: jnp.stack([f(x) for x in xs_])
    return (yp, ys[:db],
            stack(wkv_p, _blockdiag_to_pairs),
            stack(shift_p, lambda s: s[:, 0, :P_RWKV]),
            stack(conv_p, lambda s: s[:, hist_pad:]),
            stack(wkv_s, lambda s: _blockdiag_to_pairs(s[:db])),
            stack(shift_s, lambda s: s[:db, 0, :P_RWKV]),
            stack(conv_s, lambda s: s[:db, hist_pad:]))
```

```python
import functools

import jax
import jax.numpy as jnp
from jax import lax
from jax.experimental import pallas as pl
from jax.experimental.pallas import tpu as pltpu

D_MODEL = 1024
DEPTH = 4
N_META = 16
HEAD_DIM = 64
N_HEADS = 8
D_RWKV = N_HEADS * HEAD_DIM
D_CONV = D_MODEL - D_RWKV
LORA_W = 64
LORA_A = 64
LORA_G = 160
CONV_WIDTH = 31
D_FF = 4 * D_MODEL
P_RWKV = 3 * D_RWKV + LORA_W + LORA_A + LORA_G
NORM_EPS = 1e-6
LN_EPS = 1e-5
GN_EPS = 64e-5
DECAY_SCALE = 0.606531

LANES = 128
P_RWKV_PAD = 15 * LANES
LORA_G_PAD = P_RWKV_PAD - (3 * D_RWKV + LORA_W + LORA_A)
O_WA = 3 * D_RWKV
O_G = O_WA + LORA_W + LORA_A
P_IN_PAD = P_RWKV_PAD + 2 * D_CONV
CONV_HIST = 32
WKV_CHUNK = 64
PROMPT_STREAMS = 4
PROMPT_TILE = 128
SMALL_STREAMS = 11
FF_BLOCK = 1024
VMEM_LIMIT = 48 * 1024 * 1024

ROWS = {name: i for i, name in enumerate((
    "mu", "w0a0", "k_k", "k_a", "r_k", "gn_g", "gn_b", "conv_b", "cln_g", "cln_b",
    "norm_mix", "norm_ffn", "norm_final", "pad0", "pad1", "pad2"))}

F32 = jnp.float32
BF16 = jnp.bfloat16


def _dot(a, b):
    return jnp.dot(a, b, preferred_element_type=F32)


def _dot_nt(a, b):
    return lax.dot_general(a, b, (((1,), (1,)), ((), ())), preferred_element_type=F32)


def _dot_tn(a, b):
    return lax.dot_general(a, b, (((0,), (0,)), ((), ())), preferred_element_type=F32)


def _split(x):
    hi = x.astype(BF16)
    lo = (x - hi.astype(F32)).astype(BF16)
    return hi, lo


def _dot3(a, b):
    ah, al = _split(a)
    bh, bl = _split(b)
    return _dot(ah, bh) + (_dot(al, bh) + _dot(ah, bl))


def _rms(x, g):
    return x * lax.rsqrt(jnp.mean(x * x, axis=-1, keepdims=True) + NORM_EPS) * g


def _layer_spec(shape, l):
    return pl.BlockSpec((None,) + shape, lambda *_: (l,) + (0,) * len(shape))


def _head_sum(x):
    i = lax.broadcasted_iota(jnp.int32, (LANES, LANES), 0)
    j = lax.broadcasted_iota(jnp.int32, (LANES, LANES), 1)
    ones = jnp.where(((i ^ j) & HEAD_DIM) == 0, 1.0, 0.0).astype(BF16)
    outs = []
    for g in range(D_RWKV // LANES):
        hi, lo = _split(x[:, g * LANES:(g + 1) * LANES])
        outs.append(_dot(hi, ones) + _dot(lo, ones))
    return jnp.concatenate(outs, axis=1)


def _bd(x):
    low = lax.broadcasted_iota(jnp.int32, x.shape, 1) < x.shape[1] // 2
    return jnp.concatenate([jnp.where(low, x, 0.0), jnp.where(low, 0.0, x)], axis=0)


def _pair_masks(c):
    ti = lax.broadcasted_iota(jnp.int32, (c, 2 * c), 0)
    si = lax.broadcasted_iota(jnp.int32, (c, 2 * c), 1) & (c - 1)
    offs = []
    n = 1
    while n < c:
        same = ((ti ^ si) & ~(2 * n - 1)) == 0
        offs.append(same & ((ti & n) != 0) & ((si & n) == 0))
        n *= 2
    return ti > si, ti >= si, jnp.where(ti == si, 1.0, 0.0).astype(F32), offs


def _wkv_prepare(lhs, bdb, bdk, bdv, masks):
    strict, incl, eye, offs = masks
    n = range(len(lhs))
    c = lhs[0].shape[0] // 2
    gb = [_dot_nt(lhs[i], bdb[i]) for i in n]
    gk = [_dot_nt(lhs[i], bdk[i]) for i in n]
    lb = [jnp.where(strict, gb[i][:c], 0.0) for i in n]
    lkv = [_dot(jnp.where(strict, gk[i][:c], 0.0).astype(BF16), bdv[i]) for i in n]
    arb = [jnp.where(incl, gb[i][c:], 0.0).astype(BF16) for i in n]
    ark = [jnp.where(incl, gk[i][c:], 0.0).astype(BF16) for i in n]
    t = [eye - jnp.where(offs[0], lb[i], 0.0) for i in n]
    for off in offs[1:]:
        lt = [_dot(jnp.where(off, lb[i], 0.0).astype(BF16), _bd(t[i]).astype(BF16)) for i in n]
        t = [t[i] - _dot(t[i].astype(BF16), _bd(lt[i]).astype(BF16)) for i in n]
    return t, lkv, arb, ark


def _wkv_advance(s0, lhs, t, lkv, arb, ark, bdv, uv_v, uv_rhs, wc, diag):
    n = range(len(s0))
    c = lhs[0].shape[0] // 2
    x = [_dot_nt(lhs[i], s0[i].astype(BF16)) for i in n]
    u = [_dot3(t[i], _bd(-x[i][:c] - lkv[i])) for i in n]
    o = [x[i][c:] + _dot(arb[i], _bd(u[i]).astype(BF16)) + _dot(ark[i], bdv[i]) for i in n]
    upd = [_dot_tn(jnp.concatenate([u[i].astype(BF16), uv_v[i]], axis=0), uv_rhs[i]) for i in n]
    s1 = [(s0[i] + jnp.where(diag, upd[i], 0.0)) * wc[i] for i in n]
    return o, s1


def _mixer_kernel(h_ref, wkv0_ref, shift0_ref, conv0_ref, tab_ref, win_ref, wwa_ref, lg_ref, cw_ref,
                  mix_ref, wkv_ref, shift_ref, conv_ref,
                  s_ref, carry_ref, ext_ref, kk_s, rt_s, b_s, k_s, v_s, wc_s, g_s, bonus_s, o_s,
                  *, chunk, shared_init, proj_streams):
    j = pl.program_id(1)
    nb, tt, _ = h_ref.shape
    n_chunks = tt // chunk
    n_pairs = D_RWKV // LANES
    tab = lambda name, width: tab_ref[ROWS[name]:ROWS[name] + 1, 0:width]

    @pl.when(j == 0)
    def _():
        for s in range(nb):
            init = 0 if shared_init else s
            s_ref[s] = wkv0_ref[init]
            carry_ref[s, 0:1, :] = shift0_ref[init]
            ext_ref[s, 0:CONV_HIST, :] = conv0_ref[init]

    first = CONV_HIST - (CONV_WIDTH - 1)
    ti = lax.broadcasted_iota(jnp.int32, (tt, tt), 0)
    si = lax.broadcasted_iota(jnp.int32, (tt, tt), 1)
    tril = jnp.where((ti >= si) & ((ti ^ si) < chunk), 1.0, 0.0).astype(BF16)
    for s in range(nb):
        rows = slice(s * tt, (s + 1) * tt)
        if s % proj_streams == 0:
            x = h_ref[s:s + proj_streams].reshape(proj_streams * tt, D_MODEL)
            proj = _dot(_rms(x, tab("norm_mix", D_MODEL)).astype(BF16), win_ref[...])
        mine = slice((s % proj_streams) * tt, (s % proj_streams + 1) * tt)
        p = proj[mine, :P_RWKV_PAD]
        u = proj[mine, P_RWKV_PAD:P_RWKV_PAD + D_CONV] * jax.nn.sigmoid(proj[mine, P_RWKV_PAD + D_CONV:])
        row = lax.broadcasted_iota(jnp.int32, p.shape, 0)
        prev = jnp.where(row == 0, carry_ref[s, 0:1, :], pltpu.roll(p, 1, 0))
        xs = p + (prev - p) * tab("mu", P_RWKV_PAD)
        carry_ref[s, 0:1, :] = p[tt - 1:tt, :]

        r = xs[:, 0:D_RWKV]
        k = xs[:, D_RWKV:2 * D_RWKV]
        v = xs[:, 2 * D_RWKV:3 * D_RWKV]
        xwa = xs[:, O_WA:O_G]
        lane = lax.broadcasted_iota(jnp.int32, xwa.shape, 1)
        wa = (_dot(jnp.where(lane < LORA_W, jnp.tanh(xwa), xwa).astype(BF16), wwa_ref[...])
              + tab("w0a0", 2 * D_RWKV))
        logw = -DECAY_SCALE * jax.nn.sigmoid(wa[:, :D_RWKV])
        a = jax.nn.sigmoid(wa[:, D_RWKV:])
        g_s[rows, :] = _dot(jax.nn.sigmoid(xs[:, O_G:]).astype(BF16), lg_ref[...])

        kk = k * tab("k_k", D_RWKV)
        kk = kk / jnp.maximum(jnp.sqrt(_head_sum(kk * kk)), 1e-12)
        k = k * (1.0 + (a - 1.0) * tab("k_a", D_RWKV))
        bonus_s[rows, :] = _head_sum(r * k * tab("r_k", D_RWKV)) * v

        l_hi, l_lo = _split(logw)
        l_lo2 = (logw - l_hi.astype(F32) - l_lo.astype(F32)).astype(BF16)
        cl = _dot(tril, l_hi) + (_dot(tril, l_lo) + _dot(tril, l_lo2))
        w_inc = jnp.exp(cl)
        w_inv = jnp.exp(-cl)
        kk_s[rows, :] = (kk * jnp.exp(cl - logw)).astype(BF16)
        rt_s[rows, :] = (r * w_inc).astype(BF16)
        b_s[rows, :] = (kk * a * w_inv).astype(BF16)
        k_s[rows, :] = (k * w_inv).astype(BF16)
        v_s[rows, :] = v.astype(BF16)
        for c in range(n_chunks):
            wc_s[(s * n_chunks + c) * 8:(s * n_chunks + c + 1) * 8, :] = jnp.broadcast_to(
                w_inc[(c + 1) * chunk - 1:(c + 1) * chunk, :], (8, D_RWKV))

        ext_ref[s, CONV_HIST:CONV_HIST + tt, :] = u
        ext = ext_ref[s]
        acc = jnp.zeros((tt, D_CONV), F32) + tab("conv_b", D_CONV)
        for q in range(8):
            eq = ext if q == 0 else pltpu.roll(ext, CONV_HIST + tt - q, 0)
            for o in range(first + (q - first) % 8, CONV_HIST + 1, 8):
                acc = acc + eq[o - q:o - q + tt, :] * cw_ref[o - first:o - first + 1, :]
        m = jnp.mean(acc, axis=-1, keepdims=True)
        cc = acc - m
        cv = jnp.mean(cc * cc, axis=-1, keepdims=True)
        ln = cc * lax.rsqrt(cv + LN_EPS) * tab("cln_g", D_CONV) + tab("cln_b", D_CONV)
        mix_ref[s, :, D_RWKV:D_MODEL] = ln * jax.nn.sigmoid(ln)
        ext_ref[s, 0:CONV_HIST, :] = ext[tt:tt + CONV_HIST, :]

    @pl.when(j == pl.num_programs(1) - 1)
    def _():
        for s in range(nb):
            shift_ref[s] = carry_ref[s, 0:1, :]
            conv_ref[s] = ext_ref[s, 0:CONV_HIST, :]

    masks = _pair_masks(chunk)
    di = lax.broadcasted_iota(jnp.int32, (LANES, LANES), 0)
    dj = lax.broadcasted_iota(jnp.int32, (LANES, LANES), 1)
    diag = ((di ^ dj) & HEAD_DIM) == 0
    chains = [(s, p) for s in range(nb) for p in range(n_pairs)]
    cat = lambda a, b: jnp.concatenate([a, b], axis=0)
    bd16 = lambda x: _bd(x.astype(F32)).astype(BF16)
    prepared = []
    for c in range(n_chunks):
        at = lambda ref, s, p: ref[s * tt + c * chunk:s * tt + (c + 1) * chunk, p * LANES:(p + 1) * LANES]
        lhs = [cat(at(kk_s, s, p), at(rt_s, s, p)) for s, p in chains]
        bdb = [bd16(at(b_s, s, p)) for s, p in chains]
        bdk = [bd16(at(k_s, s, p)) for s, p in chains]
        bdv = [bd16(at(v_s, s, p)) for s, p in chains]
        prepared.append((lhs, bdv) + tuple(_wkv_prepare(lhs, bdb, bdk, bdv, masks)))
    for c in range(n_chunks):
        at = lambda ref, s, p: ref[s * tt + c * chunk:s * tt + (c + 1) * chunk, p * LANES:(p + 1) * LANES]
        lhs, bdv, t, lkv, arb, ark = prepared[c]
        uv_v = [at(v_s, s, p) for s, p in chains]
        uv_rhs = [cat(at(b_s, s, p), at(k_s, s, p)) for s, p in chains]
        wc = [wc_s[(s * n_chunks + c) * 8:(s * n_chunks + c) * 8 + 1, p * LANES:(p + 1) * LANES]
              for s, p in chains]
        o, s1 = _wkv_advance([s_ref[s, p] for s, p in chains], lhs, t, lkv, arb, ark, bdv, uv_v, uv_rhs, wc, diag)
        for i, (s, p) in enumerate(chains):
            o_s[s * tt + c * chunk:s * tt + (c + 1) * chunk, p * LANES:(p + 1) * LANES] = o[i]
            s_ref[s, p] = s1[i]

    streams = range(nb)
    rows = [slice(s * tt, (s + 1) * tt) for s in streams]
    mean = [_head_sum(o_s[rows[s], :]) * (1.0 / HEAD_DIM) for s in streams]
    oc = [o_s[rows[s], :] - mean[s] for s in streams]
    var = [_head_sum(oc[s] * oc[s]) * (1.0 / HEAD_DIM) for s in streams]
    for s in streams:
        mix_ref[s, :, 0:D_RWKV] = (oc[s] * lax.rsqrt(var[s] + GN_EPS) * tab("gn_g", D_RWKV) + tab("gn_b", D_RWKV)
                                   + bonus_s[rows[s], :]) * g_s[rows[s], :]

    @pl.when(j == pl.num_programs(1) - 1)
    def _():
        wkv_ref[...] = s_ref[...]


def _mixer(h, wkv0, shift0, conv0, pp, l, nb, tt, chunk, proj_streams):
    b, t, _ = h.shape
    shared = wkv0.shape[0] != b
    n_pairs = D_RWKV // LANES

    def st(*tail):
        if shared:
            return pl.BlockSpec((1,) + tail, lambda i, j: (0,) * (1 + len(tail)))
        return pl.BlockSpec((nb,) + tail, lambda i, j: (i,) + (0,) * len(tail))

    rows = nb * tt
    bf = pltpu.VMEM((rows, D_RWKV), BF16)
    f32 = pltpu.VMEM((rows, D_RWKV), F32)
    kern = functools.partial(_mixer_kernel, chunk=chunk, shared_init=shared, proj_streams=proj_streams)
    return pl.pallas_call(
        kern,
        grid=(b // nb, t // tt),
        in_specs=[pl.BlockSpec((nb, tt, D_MODEL), lambda i, j: (i, j, 0)),
                  st(n_pairs, LANES, LANES), st(1, P_RWKV_PAD), st(CONV_HIST, D_CONV),
                  _layer_spec((len(ROWS), P_RWKV_PAD), l),
                  _layer_spec((D_MODEL, P_IN_PAD), l),
                  _layer_spec((LORA_W + LORA_A, 2 * D_RWKV), l),
                  _layer_spec((LORA_G_PAD, D_RWKV), l),
                  _layer_spec((CONV_HIST, D_CONV), l)],
        out_specs=[pl.BlockSpec((nb, tt, D_MODEL), lambda i, j: (i, j, 0)),
                   pl.BlockSpec((nb, n_pairs, LANES, LANES), lambda i, j: (i, 0, 0, 0)),
                   pl.BlockSpec((nb, 1, P_RWKV_PAD), lambda i, j: (i, 0, 0)),
                   pl.BlockSpec((nb, CONV_HIST, D_CONV), lambda i, j: (i, 0, 0))],
        out_shape=[jax.ShapeDtypeStruct((b, t, D_MODEL), F32),
                   jax.ShapeDtypeStruct((b, n_pairs, LANES, LANES), F32),
                   jax.ShapeDtypeStruct((b, 1, P_RWKV_PAD), F32),
                   jax.ShapeDtypeStruct((b, CONV_HIST, D_CONV), F32)],
        scratch_shapes=[pltpu.VMEM((nb, n_pairs, LANES, LANES), F32),
                        pltpu.VMEM((nb, 8, P_RWKV_PAD), F32),
                        pltpu.VMEM((nb, CONV_HIST + tt, D_CONV), F32),
                        bf, bf, bf, bf, bf,
                        pltpu.VMEM((8 * nb * (tt // chunk), D_RWKV), F32),
                        f32, f32, f32],
        compiler_params=pltpu.CompilerParams(dimension_semantics=("parallel", "arbitrary"),
                                             vmem_limit_bytes=VMEM_LIMIT),
        name="mixer",
    )(h, wkv0, shift0, conv0, pp["table"], pp["w_in"], pp["wwa"], pp["lg"], pp["conv_w"])


def _pairs_to_blockdiag(s):
    z = jnp.zeros_like(s[:, 0::2])
    top = jnp.concatenate([s[:, 0::2], z], axis=-1)
    bot = jnp.concatenate([z, s[:, 1::2]], axis=-1)
    return jnp.concatenate([top, bot], axis=-2)


def _blockdiag_to_pairs(s):
    even = s[:, :, :HEAD_DIM, :HEAD_DIM]
    odd = s[:, :, HEAD_DIM:, HEAD_DIM:]
    return jnp.stack([even, odd], axis=2).reshape(s.shape[0], N_HEADS, HEAD_DIM, HEAD_DIM)


def _ffn_kernel(h_ref, mix_ref, tab_ref, wo_ref, wu_ref, wd_ref, out_ref, *, final):
    tab = lambda name: tab_ref[ROWS[name]:ROWS[name] + 1, 0:D_MODEL]
    h1 = h_ref[...] + _dot(mix_ref[...].astype(BF16), wo_ref[...])
    hn = _rms(h1, tab("norm_ffn")).astype(BF16)
    acc = h1
    for c in range(D_FF // FF_BLOCK):
        cols = slice(c * FF_BLOCK, (c + 1) * FF_BLOCK)
        up = jnp.maximum(_dot(hn, wu_ref[:, cols]), 0.0)
        acc = acc + _dot((up * up).astype(BF16), wd_ref[cols, :])
    out_ref[...] = _rms(acc, tab("norm_final")) if final else acc


def _ffn(h, mix, pp, l, tm, final):
    n = h.shape[0]
    tile = pl.BlockSpec((tm, D_MODEL), lambda i: (i, 0))
    return pl.pallas_call(
        functools.partial(_ffn_kernel, final=final),
        grid=(n // tm,),
        in_specs=[tile, tile,
                  _layer_spec((len(ROWS), P_RWKV_PAD), l),
                  _layer_spec((D_MODEL, D_MODEL), l),
                  _layer_spec((D_MODEL, D_FF), l),
                  _layer_spec((D_FF, D_MODEL), l)],
        out_specs=tile,
        out_shape=jax.ShapeDtypeStruct((n, D_MODEL), F32),
        compiler_params=pltpu.CompilerParams(dimension_semantics=("parallel",),
                                             vmem_limit_bytes=VMEM_LIMIT),
        name="ffn",
    )(h, mix, pp["table"], pp["w_out"], pp["w_up"], pp["w_down"])


def _prepare_params(norm_mix, w_in, mu_shift, w0, lora_w, a0, lora_a, lora_g, k_k, k_a, r_k, gn_g, gn_b,
                    conv_w, conv_b, cln_g, cln_b, w_out, norm_ffn, w_up, w_down, norm_final):
    depth = w_in.shape[0]
    pad_last = lambda x, n: jnp.pad(x, [(0, 0)] * (x.ndim - 1) + [(0, n - x.shape[-1])])
    vec = lambda x: pad_last(x.reshape(depth, -1).astype(F32), P_RWKV_PAD)
    rows = dict(mu=mu_shift, w0a0=jnp.concatenate([w0, a0], axis=-1), k_k=k_k, k_a=k_a, r_k=r_k, gn_g=gn_g,
                gn_b=gn_b, conv_b=conv_b, cln_g=cln_g, cln_b=cln_b, norm_mix=norm_mix, norm_ffn=norm_ffn,
                norm_final=jnp.broadcast_to(norm_final, (depth, D_MODEL)))
    zero = jnp.zeros((depth, P_RWKV_PAD), F32)
    table = jnp.stack([vec(rows[name]) if name in rows else zero for name in ROWS], axis=1)
    w_in_p = jnp.concatenate([pad_last(w_in[..., :P_RWKV], P_RWKV_PAD), w_in[..., P_RWKV:]], axis=-1).astype(BF16)
    zeros = jnp.zeros_like(lora_w)
    wwa = jnp.concatenate([jnp.concatenate([lora_w, zeros], axis=-1),
                           jnp.concatenate([zeros, lora_a], axis=-1)], axis=-2).astype(BF16)
    return dict(
        table=table, w_in=w_in_p, wwa=wwa,
        lg=jnp.pad(lora_g, ((0, 0), (0, LORA_G_PAD - LORA_G), (0, 0))).astype(BF16),
        conv_w=jnp.pad(conv_w, ((0, 0), (0, CONV_HIST - CONV_WIDTH), (0, 0))),
        w_out=w_out.astype(BF16), w_up=w_up.astype(BF16), w_down=w_down.astype(BF16))


def _run_group(x, wkv0, shift0, conv0, pp, tm, nb, tt, chunk, proj_streams):
    b, t, _ = x.shape
    h = x
    wkv, shift, conv = [], [], []
    for l in range(DEPTH):
        mix, s1, s2, s3 = _mixer(h, wkv0[l], shift0[l], conv0[l], pp, l, nb, tt, chunk, proj_streams)
        h = _ffn(h.reshape(b * t, D_MODEL), mix.reshape(b * t, D_MODEL), pp, l, tm,
                 final=(l == DEPTH - 1)).reshape(b, t, D_MODEL)
        wkv.append(s1)
        shift.append(s2)
        conv.append(s3)
    return h, wkv, shift, conv


def kernel(x_prompt, x_sample, state_wkv, state_shift, cache_conv, meta_tokens, norm_mix, w_in, mu_shift, w0, lora_w, a0, lora_a, lora_g, k_k, k_a, r_k, gn_g, gn_b, conv_w, conv_b, cln_g, cln_b, w_out, norm_ffn, w_up, w_down, norm_final):
    pp = _prepare_params(norm_mix, w_in, mu_shift, w0, lora_w, a0, lora_a, lora_g, k_k, k_a, r_k, gn_g, gn_b,
                         conv_w, conv_b, cln_g, cln_b, w_out, norm_ffn, w_up, w_down, norm_final)
    db, dt, _ = x_sample.shape
    hist_pad = CONV_HIST - (CONV_WIDTH - 1)

    xs = jnp.concatenate([x_sample, meta_tokens[None].astype(x_sample.dtype)], axis=0)
    nb = db + 1
    wkv0 = [_pairs_to_blockdiag(jnp.pad(w, ((0, 1), (0, 0), (0, 0), (0, 0)))) for w in state_wkv]
    shift0 = jnp.pad(state_shift, ((0, 0), (0, 1), (0, P_RWKV_PAD - P_RWKV)))[:, :, None, :]
    conv0 = jnp.pad(cache_conv, ((0, 0), (0, 1), (hist_pad, 0), (0, 0)))
    ys, wkv_s, shift_s, conv_s = _run_group(xs, wkv0, list(shift0), list(conv0), pp,
                                            tm=nb * dt, nb=SMALL_STREAMS, tt=dt, chunk=dt, proj_streams=SMALL_STREAMS)
    yp, wkv_p, shift_p, conv_p = _run_group(
        x_prompt, [s[db:] for s in wkv_s], [s[db:] for s in shift_s], [s[db:] for s in conv_s],
        pp, tm=512, nb=PROMPT_STREAMS, tt=PROMPT_TILE, chunk=WKV_CHUNK, proj_streams=1)

    stack = lambda xs_, f: jnp.stack([f(x) for x in xs_])
    return (yp, ys[:db],
            stack(wkv_p, _blockdiag_to_pairs),
            stack(shift_p, lambda s: s[:, 0, :P_RWKV]),
            stack(conv_p, lambda s: s[:, hist_pad:]),
            stack(wkv_s, lambda s: _blockdiag_to_pairs(s[:db])),
            stack(shift_s, lambda s: s[:db, 0, :P_RWKV]),
            stack(conv_s, lambda s: s[:db, hist_pad:]))
```

```python
import functools

import jax
import jax.numpy as jnp
from jax import lax
from jax.experimental import pallas as pl
from jax.experimental.pallas import tpu as pltpu

D_MODEL = 1024
DEPTH = 4
N_META = 16
HEAD_DIM = 64
N_HEADS = 8
D_RWKV = N_HEADS * HEAD_DIM
D_CONV = D_MODEL - D_RWKV
LORA_W = 64
LORA_A = 64
LORA_G = 160
CONV_WIDTH = 31
D_FF = 4 * D_MODEL
P_RWKV = 3 * D_RWKV + LORA_W + LORA_A + LORA_G
NORM_EPS = 1e-6
LN_EPS = 1e-5
GN_EPS = 64e-5
DECAY_SCALE = 0.606531

LANES = 128
P_RWKV_PAD = 15 * LANES
LORA_G_PAD = P_RWKV_PAD - (3 * D_RWKV + LORA_W + LORA_A)
O_WA = 3 * D_RWKV
O_G = O_WA + LORA_W + LORA_A
CONV_HIST = 32
WKV_CHUNK = 64
PROMPT_STREAMS = 4
PROMPT_TILE = 128
SMALL_STREAMS = 11
FF_BLOCK = 1024
VMEM_LIMIT = 48 * 1024 * 1024

ROWS = {name: i for i, name in enumerate((
    "mu", "w0a0", "k_k", "k_a", "r_k", "gn_g", "gn_b", "conv_b", "cln_g", "cln_b",
    "norm_mix", "norm_ffn", "norm_final", "pad0", "pad1", "pad2"))}

F32 = jnp.float32
BF16 = jnp.bfloat16


def _dot(a, b):
    return jnp.dot(a, b, preferred_element_type=F32)


def _dot_nt(a, b):
    return lax.dot_general(a, b, (((1,), (1,)), ((), ())), preferred_element_type=F32)


def _dot_tn(a, b):
    return lax.dot_general(a, b, (((0,), (0,)), ((), ())), preferred_element_type=F32)


def _split(x):
    hi = x.astype(BF16)
    lo = (x - hi.astype(F32)).astype(BF16)
    return hi, lo


def _rms(x, g):
    return x * lax.rsqrt(jnp.mean(x * x, axis=-1, keepdims=True) + NORM_EPS) * g


def _layer_spec(shape, l):
    return pl.BlockSpec((None,) + shape, lambda *_: (l,) + (0,) * len(shape))


def _head_sum(x):
    i = lax.broadcasted_iota(jnp.int32, (LANES, LANES), 0)
    j = lax.broadcasted_iota(jnp.int32, (LANES, LANES), 1)
    ones = jnp.where(((i ^ j) & HEAD_DIM) == 0, 1.0, 0.0).astype(BF16)
    outs = []
    for g in range(D_RWKV // LANES):
        hi, lo = _split(x[:, g * LANES:(g + 1) * LANES])
        outs.append(_dot(hi, ones) + _dot(lo, ones))
    return jnp.concatenate(outs, axis=1)


def _bd(x):
    low = lax.broadcasted_iota(jnp.int32, x.shape, 1) < x.shape[1] // 2
    return jnp.concatenate([jnp.where(low, x, 0.0), jnp.where(low, 0.0, x)], axis=0)


def _pair_masks(c):
    ti = lax.broadcasted_iota(jnp.int32, (c, 2 * c), 0)
    si = lax.broadcasted_iota(jnp.int32, (c, 2 * c), 1) & (c - 1)
    offs = []
    n = 1
    while n < c:
        same = ((ti ^ si) & ~(2 * n - 1)) == 0
        offs.append(same & ((ti & n) != 0) & ((si & n) == 0))
        n *= 2
    return ti > si, ti >= si, jnp.where(ti == si, 1.0, 0.0).astype(F32), offs


def _zero_token(x):
    tot = None
    for r0 in range(0, x.shape[0], 8):
        for c0 in range(0, x.shape[1], LANES):
            blk = x[r0:r0 + 8, c0:c0 + LANES]
            tot = blk if tot is None else tot + blk
    bits = lax.shift_right_logical(pltpu.bitcast(tot, jnp.uint32), jnp.uint32(32))
    return pltpu.bitcast(bits, F32)


def _pin(xs, sync):
    token = sync.pop("token", None)
    if token is None:
        return xs
    return [jnp.concatenate([xs[0][0:8] + token[:, :xs[0].shape[1]], xs[0][8:]], axis=0)] + xs[1:]


def _wkv_prepare(lhs, bdb, bdk, bdv, masks, sync):
    strict, incl, eye, offs = masks
    n = range(len(lhs))
    c = lhs[0].shape[0] // 2
    gb = [_dot_nt(lhs[i], bdb[i]) for i in n]
    yield
    gk = [_dot_nt(lhs[i], bdk[i]) for i in n]
    yield
    lb = [jnp.where(strict, gb[i][:c], 0.0) for i in n]
    arb = [jnp.where(incl, gb[i][c:], 0.0).astype(BF16) for i in n]
    both = jnp.concatenate([strict, incl], axis=0)
    gkm = _pin([jnp.where(both, gk[i], 0.0) for i in n], sync)
    kv = [_dot(gkm[i].astype(BF16), bdv[i]) for i in n]
    yield
    t = [eye - jnp.where(offs[0], lb[i], 0.0) for i in n]
    for off in offs[1:]:
        t = _pin(t, sync)
        lt = [_dot(jnp.where(off, lb[i], 0.0).astype(BF16), _bd(t[i]).astype(BF16)) for i in n]
        yield
        lt = _pin(lt, sync)
        t = [t[i] - _dot(t[i].astype(BF16), _bd(lt[i]).astype(BF16)) for i in n]
        yield
    return t, kv, arb


def _wkv_advance(s0, lhs, t, kv, arb, uv_v, uv_rhs, wc, diag, sync):
    n = range(len(s0))
    c = lhs[0].shape[0] // 2
    s0p = _pin(list(s0), sync)
    x = [_dot_nt(lhs[i], s0p[i].astype(BF16)) for i in n]
    yield
    th = [_split(t[i]) for i in n]
    rhs = _pin([-x[i][:c] - kv[i][:c] for i in n], sync)
    rh = [_split(_bd(rhs[i])) for i in n]
    uu = [_dot(jnp.concatenate(th[i], axis=0), rh[i][0]) for i in n]
    u = [uu[i][:c] + (uu[i][c:] + _dot(th[i][0], rh[i][1])) for i in n]
    yield
    u = _pin(u, sync)
    o = [x[i][c:] + _dot(arb[i], _bd(u[i]).astype(BF16)) + kv[i][c:] for i in n]
    yield
    upd = [_dot_tn(jnp.concatenate([u[i].astype(BF16), uv_v[i]], axis=0), uv_rhs[i]) for i in n]
    s1 = [(s0[i] + jnp.where(diag, upd[i], 0.0)) * wc[i] for i in n]
    yield
    return o, s1


def _mixer_kernel(h_ref, wkv0_ref, shift0_ref, conv0_ref, tab_ref, wr_ref, wc_ref, wwa_ref, lg_ref, cw_ref,
                  mix_ref, wkv_ref, shift_ref, conv_ref,
                  s_ref, carry_ref, ext_ref, kk_s, rt_s, b_s, k_s, v_s, wc_s, g_s, bonus_s, o_s,
                  *, chunk, shared_init, proj_streams):
    j = pl.program_id(1)
    nb, tt, _ = h_ref.shape
    n_chunks = tt // chunk
    n_pairs = D_RWKV // LANES
    tab = lambda name, width: tab_ref[ROWS[name]:ROWS[name] + 1, 0:width]

    @pl.when(j == 0)
    def _():
        for s in range(nb):
            init = 0 if shared_init else s
            s_ref[s] = jnp.zeros(s_ref.shape[1:], F32)
            for p in range(n_pairs):
                s_ref[s, p, 0:HEAD_DIM, 0:HEAD_DIM] = wkv0_ref[init, 2 * p]
                s_ref[s, p, HEAD_DIM:LANES, HEAD_DIM:LANES] = wkv0_ref[init, 2 * p + 1]
            carry_ref[s, 0:1, :] = shift0_ref[init]
            ext_ref[s, 0:CONV_HIST, :] = conv0_ref[init]

    first = CONV_HIST - (CONV_WIDTH - 1)
    ti = lax.broadcasted_iota(jnp.int32, (tt, tt), 0)
    si = lax.broadcasted_iota(jnp.int32, (tt, tt), 1)
    tril = jnp.where((ti >= si) & ((ti ^ si) < chunk), 1.0, 0.0).astype(BF16)
    for s in range(nb):
        rows = slice(s * tt, (s + 1) * tt)
        if s % proj_streams == 0:
            x = h_ref[s:s + proj_streams].reshape(proj_streams * tt, D_MODEL)
            xn = _rms(x, tab("norm_mix", D_MODEL)).astype(BF16)
            proj_r = _dot(xn, wr_ref[...])
            proj_c = _dot(xn, wc_ref[...])
        mine = slice((s % proj_streams) * tt, (s % proj_streams + 1) * tt)
        p = proj_r[mine]
        u = proj_c[mine, :D_CONV] * jax.nn.sigmoid(proj_c[mine, D_CONV:])
        row = lax.broadcasted_iota(jnp.int32, p.shape, 0)
        prev = jnp.where(row == 0, carry_ref[s, 0:1, :], pltpu.roll(p, 1, 0))
        xs = p + (prev - p) * tab("mu", P_RWKV_PAD)
        carry_ref[s, 0:1, :] = p[tt - 1:tt, :]

        r = xs[:, 0:D_RWKV]
        k = xs[:, D_RWKV:2 * D_RWKV]
        v = xs[:, 2 * D_RWKV:3 * D_RWKV]
        xwa = xs[:, O_WA:O_G]
        lane = lax.broadcasted_iota(jnp.int32, xwa.shape, 1)
        wa = (_dot(jnp.where(lane < LORA_W, jnp.tanh(xwa), xwa).astype(BF16), wwa_ref[...])
              + tab("w0a0", 2 * D_RWKV))
        logw = -DECAY_SCALE * jax.nn.sigmoid(wa[:, :D_RWKV])
        a = jax.nn.sigmoid(wa[:, D_RWKV:])
        g_s[rows, :] = _dot(jax.nn.sigmoid(xs[:, O_G:]).astype(BF16), lg_ref[...])

        kk = k * tab("k_k", D_RWKV)
        kk = kk / jnp.maximum(jnp.sqrt(_head_sum(kk * kk)), 1e-12)
        k = k * (1.0 + (a - 1.0) * tab("k_a", D_RWKV))
        bonus_s[rows, :] = _head_sum(r * k * tab("r_k", D_RWKV)) * v

        l_hi, l_lo = _split(logw)
        l_lo2 = (logw - l_hi.astype(F32) - l_lo.astype(F32)).astype(BF16)
        cl = _dot(tril, l_hi) + (_dot(tril, l_lo) + _dot(tril, l_lo2))
        w_inc = jnp.exp(cl)
        w_inv = jnp.exp(-cl)
        kk_s[rows, :] = (kk * jnp.exp(cl - logw)).astype(BF16)
        rt_s[rows, :] = (r * w_inc).astype(BF16)
        b_s[rows, :] = (kk * a * w_inv).astype(BF16)
        k_s[rows, :] = (k * w_inv).astype(BF16)
        v_s[rows, :] = v.astype(BF16)
        for c in range(n_chunks):
            wc_s[(s * n_chunks + c) * 8:(s * n_chunks + c + 1) * 8, :] = jnp.broadcast_to(
                w_inc[(c + 1) * chunk - 1:(c + 1) * chunk, :], (8, D_RWKV))

        ext_ref[s, CONV_HIST:CONV_HIST + tt, :] = u

    sync = {}
    conv_rows = min(tt, CONV_HIST)

    def conv_pieces():
        for s in range(nb):
            for r0 in range(0, tt, conv_rows):
                blk = ext_ref[s, r0:r0 + conv_rows + CONV_HIST, :]
                acc = jnp.zeros((conv_rows, D_CONV), F32) + tab("conv_b", D_CONV)
                for q in range(8):
                    eq = blk if q == 0 else pltpu.roll(blk, conv_rows + CONV_HIST - q, 0)
                    for o in range(first + (q - first) % 8, CONV_HIST + 1, 8):
                        acc = acc + eq[o - q:o - q + conv_rows, :] * cw_ref[o - first:o - first + 1, :]
                    sync["token"] = _zero_token(acc)
                    yield
                m = jnp.mean(acc, axis=-1, keepdims=True)
                cc = acc - m
                cv = jnp.mean(cc * cc, axis=-1, keepdims=True)
                ln = cc * lax.rsqrt(cv + LN_EPS) * tab("cln_g", D_CONV) + tab("cln_b", D_CONV)
                mix_ref[s, r0:r0 + conv_rows, D_RWKV:D_MODEL] = ln * jax.nn.sigmoid(ln)
                yield

    def recurrence():
        masks = _pair_masks(chunk)
        di = lax.broadcasted_iota(jnp.int32, (LANES, LANES), 0)
        dj = lax.broadcasted_iota(jnp.int32, (LANES, LANES), 1)
        diag = ((di ^ dj) & HEAD_DIM) == 0
        chains = [(s, p) for s in range(nb) for p in range(n_pairs)]
        cat = lambda a, b: jnp.concatenate([a, b], axis=0)
        bd16 = lambda x: _bd(x.astype(F32)).astype(BF16)
        prepared = []
        for c in range(n_chunks):
            at = lambda ref, s, p: ref[s * tt + c * chunk:s * tt + (c + 1) * chunk, p * LANES:(p + 1) * LANES]
            lhs = [cat(at(kk_s, s, p), at(rt_s, s, p)) for s, p in chains]
            bdb = [bd16(at(b_s, s, p)) for s, p in chains]
            bdk = [bd16(at(k_s, s, p)) for s, p in chains]
            bdv = [bd16(at(v_s, s, p)) for s, p in chains]
            prepared.append((lhs,) + tuple((yield from _wkv_prepare(lhs, bdb, bdk, bdv, masks, sync))))
        for c in range(n_chunks):
            at = lambda ref, s, p: ref[s * tt + c * chunk:s * tt + (c + 1) * chunk, p * LANES:(p + 1) * LANES]
            lhs, t, kv, arb = prepared[c]
            uv_v = [at(v_s, s, p) for s, p in chains]
            uv_rhs = [cat(at(b_s, s, p), at(k_s, s, p)) for s, p in chains]
            wc = [wc_s[(s * n_chunks + c) * 8:(s * n_chunks + c) * 8 + 1, p * LANES:(p + 1) * LANES]
                  for s, p in chains]
            o, s1 = yield from _wkv_advance([s_ref[s, p] for s, p in chains], lhs, t, kv, arb, uv_v, uv_rhs,
                                            wc, diag, sync)
            for i, (s, p) in enumerate(chains):
                o_s[s * tt + c * chunk:s * tt + (c + 1) * chunk, p * LANES:(p + 1) * LANES] = o[i]
                s_ref[s, p] = s1[i]

    conv = conv_pieces()
    n_conv = nb * (tt // conv_rows) * 9
    n_stages = n_chunks * (3 + 2 * (len(_pair_masks(chunk)[3]) - 1) + 4)
    per_stage = -(-n_conv // n_stages)
    for _ in recurrence():
        for _ in range(per_stage):
            next(conv, None)
    for _ in conv:
        pass
    for s in range(nb):
        ext_ref[s, 0:CONV_HIST, :] = ext_ref[s, tt:tt + CONV_HIST, :]

    streams = range(nb)
    rows = [slice(s * tt, (s + 1) * tt) for s in streams]
    mean = [_head_sum(o_s[rows[s], :]) * (1.0 / HEAD_DIM) for s in streams]
    oc = [o_s[rows[s], :] - mean[s] for s in streams]
    var = [_head_sum(oc[s] * oc[s]) * (1.0 / HEAD_DIM) for s in streams]
    for s in streams:
        mix_ref[s, :, 0:D_RWKV] = (oc[s] * lax.rsqrt(var[s] + GN_EPS) * tab("gn_g", D_RWKV) + tab("gn_b", D_RWKV)
                                   + bonus_s[rows[s], :]) * g_s[rows[s], :]

    @pl.when(j == pl.num_programs(1) - 1)
    def _():
        for s in range(nb):
            shift_ref[s] = carry_ref[s, 0:1, :]
            conv_ref[s] = ext_ref[s, 0:CONV_HIST, :]
            for p in range(n_pairs):
                wkv_ref[s, 2 * p] = s_ref[s, p, 0:HEAD_DIM, 0:HEAD_DIM]
                wkv_ref[s, 2 * p + 1] = s_ref[s, p, HEAD_DIM:LANES, HEAD_DIM:LANES]


def _mixer(h, wkv0, shift0, conv0, pp, l, nb, tt, chunk, proj_streams):
    b, t, _ = h.shape
    shared = wkv0.shape[0] != b
    n_pairs = D_RWKV // LANES

    def st(*tail):
        if shared:
            return pl.BlockSpec((1,) + tail, lambda i, j: (0,) * (1 + len(tail)))
        return pl.BlockSpec((nb,) + tail, lambda i, j: (i,) + (0,) * len(tail))

    rows = nb * tt
    bf = pltpu.VMEM((rows, D_RWKV), BF16)
    f32 = pltpu.VMEM((rows, D_RWKV), F32)
    kern = functools.partial(_mixer_kernel, chunk=chunk, shared_init=shared, proj_streams=proj_streams)
    return pl.pallas_call(
        kern,
        grid=(b // nb, t // tt),
        in_specs=[pl.BlockSpec((nb, tt, D_MODEL), lambda i, j: (i, j, 0)),
                  st(N_HEADS, HEAD_DIM, HEAD_DIM), st(1, P_RWKV_PAD), st(CONV_HIST, D_CONV),
                  _layer_spec((len(ROWS), P_RWKV_PAD), l),
                  _layer_spec((D_MODEL, P_RWKV_PAD), l),
                  _layer_spec((D_MODEL, 2 * D_CONV), l),
                  _layer_spec((LORA_W + LORA_A, 2 * D_RWKV), l),
                  _layer_spec((LORA_G_PAD, D_RWKV), l),
                  _layer_spec((CONV_HIST, D_CONV), l)],
        out_specs=[pl.BlockSpec((nb, tt, D_MODEL), lambda i, j: (i, j, 0)),
                   pl.BlockSpec((nb, N_HEADS, HEAD_DIM, HEAD_DIM), lambda i, j: (i, 0, 0, 0)),
                   pl.BlockSpec((nb, 1, P_RWKV_PAD), lambda i, j: (i, 0, 0)),
                   pl.BlockSpec((nb, CONV_HIST, D_CONV), lambda i, j: (i, 0, 0))],
        out_shape=[jax.ShapeDtypeStruct((b, t, D_MODEL), F32),
                   jax.ShapeDtypeStruct((b, N_HEADS, HEAD_DIM, HEAD_DIM), F32),
                   jax.ShapeDtypeStruct((b, 1, P_RWKV_PAD), F32),
                   jax.ShapeDtypeStruct((b, CONV_HIST, D_CONV), F32)],
        scratch_shapes=[pltpu.VMEM((nb, n_pairs, LANES, LANES), F32),
                        pltpu.VMEM((nb, 8, P_RWKV_PAD), F32),
                        pltpu.VMEM((nb, CONV_HIST + tt, D_CONV), F32),
                        bf, bf, bf, bf, bf,
                        pltpu.VMEM((8 * nb * (tt // chunk), D_RWKV), F32),
                        f32, f32, f32],
        compiler_params=pltpu.CompilerParams(dimension_semantics=("parallel", "arbitrary"),
                                             vmem_limit_bytes=VMEM_LIMIT),
        name="mixer",
    )(h, wkv0, shift0, conv0, pp["table"], pp["w_r"], pp["w_c"], pp["wwa"], pp["lg"], pp["conv_w"])


def _ffn_kernel(h_ref, mix_ref, tab_ref, wo_ref, wu_ref, wd_ref, out_ref, *, final):
    tab = lambda name: tab_ref[ROWS[name]:ROWS[name] + 1, 0:D_MODEL]
    h1 = h_ref[...] + _dot(mix_ref[...].astype(BF16), wo_ref[...])
    hn = _rms(h1, tab("norm_ffn")).astype(BF16)
    acc = h1
    for c in range(D_FF // FF_BLOCK):
        cols = slice(c * FF_BLOCK, (c + 1) * FF_BLOCK)
        up = jnp.maximum(_dot(hn, wu_ref[:, cols]), 0.0)
        acc = acc + _dot((up * up).astype(BF16), wd_ref[cols, :])
    out_ref[...] = _rms(acc, tab("norm_final")) if final else acc


def _ffn(h, mix, pp, l, tm, final):
    n = h.shape[0]
    tile = pl.BlockSpec((tm, D_MODEL), lambda i: (i, 0))
    return pl.pallas_call(
        functools.partial(_ffn_kernel, final=final),
        grid=(n // tm,),
        in_specs=[tile, tile,
                  _layer_spec((len(ROWS), P_RWKV_PAD), l),
                  _layer_spec((D_MODEL, D_MODEL), l),
                  _layer_spec((D_MODEL, D_FF), l),
                  _layer_spec((D_FF, D_MODEL), l)],
        out_specs=tile,
        out_shape=jax.ShapeDtypeStruct((n, D_MODEL), F32),
        compiler_params=pltpu.CompilerParams(dimension_semantics=("parallel",),
                                             vmem_limit_bytes=VMEM_LIMIT),
        name="ffn",
    )(h, mix, pp["table"], pp["w_out"], pp["w_up"], pp["w_down"])


def _prepare_params(norm_mix, w_in, mu_shift, w0, lora_w, a0, lora_a, lora_g, k_k, k_a, r_k, gn_g, gn_b,
                    conv_w, conv_b, cln_g, cln_b, w_out, norm_ffn, w_up, w_down, norm_final):
    depth = w_in.shape[0]
    pad_last = lambda x, n: jnp.pad(x, [(0, 0)] * (x.ndim - 1) + [(0, n - x.shape[-1])])
    vec = lambda x: pad_last(x.reshape(depth, -1).astype(F32), P_RWKV_PAD)
    rows = dict(mu=mu_shift, w0a0=jnp.concatenate([w0, a0], axis=-1), k_k=k_k, k_a=k_a, r_k=r_k, gn_g=gn_g,
                gn_b=gn_b, conv_b=conv_b, cln_g=cln_g, cln_b=cln_b, norm_mix=norm_mix, norm_ffn=norm_ffn,
                norm_final=jnp.broadcast_to(norm_final, (depth, D_MODEL)))
    zero = jnp.zeros((depth, P_RWKV_PAD), F32)
    table = jnp.stack([vec(rows[name]) if name in rows else zero for name in ROWS], axis=1)
    w_r = pad_last(w_in[..., :P_RWKV], P_RWKV_PAD).astype(BF16)
    w_c = w_in[..., P_RWKV:].astype(BF16)
    zeros = jnp.zeros_like(lora_w)
    wwa = jnp.concatenate([jnp.concatenate([lora_w, zeros], axis=-1),
                           jnp.concatenate([zeros, lora_a], axis=-1)], axis=-2).astype(BF16)
    return dict(
        table=table, w_r=w_r, w_c=w_c, wwa=wwa,
        lg=jnp.pad(lora_g, ((0, 0), (0, LORA_G_PAD - LORA_G), (0, 0))).astype(BF16),
        conv_w=jnp.pad(conv_w, ((0, 0), (0, CONV_HIST - CONV_WIDTH), (0, 0))),
        w_out=w_out.astype(BF16), w_up=w_up.astype(BF16), w_down=w_down.astype(BF16))


def _run_group(x, wkv0, shift0, conv0, pp, tm, nb, tt, chunk, proj_streams):
    b, t, _ = x.shape
    h = x
    wkv, shift, conv = [], [], []
    for l in range(DEPTH):
        mix, s1, s2, s3 = _mixer(h, wkv0[l], shift0[l], conv0[l], pp, l, nb, tt, chunk, proj_streams)
        h = _ffn(h.reshape(b * t, D_MODEL), mix.reshape(b * t, D_MODEL), pp, l, tm,
                 final=(l == DEPTH - 1)).reshape(b, t, D_MODEL)
        wkv.append(s1)
        shift.append(s2)
        conv.append(s3)
    return h, wkv, shift, conv


def kernel(x_prompt, x_sample, state_wkv, state_shift, cache_conv, meta_tokens, norm_mix, w_in, mu_shift, w0, lora_w, a0, lora_a, lora_g, k_k, k_a, r_k, gn_g, gn_b, conv_w, conv_b, cln_g, cln_b, w_out, norm_ffn, w_up, w_down, norm_final):
    pp = _prepare_params(norm_mix, w_in, mu_shift, w0, lora_w, a0, lora_a, lora_g, k_k, k_a, r_k, gn_g, gn_b,
                         conv_w, conv_b, cln_g, cln_b, w_out, norm_ffn, w_up, w_down, norm_final)
    db, dt, _ = x_sample.shape
    hist_pad = CONV_HIST - (CONV_WIDTH - 1)

    xs = jnp.concatenate([x_sample, meta_tokens[None].astype(x_sample.dtype)], axis=0)
    nb = db + 1
    wkv0 = jnp.pad(state_wkv, ((0, 0), (0, 1), (0, 0), (0, 0), (0, 0)))
    shift0 = jnp.pad(state_shift, ((0, 0), (0, 1), (0, P_RWKV_PAD - P_RWKV)))[:, :, None, :]
    conv0 = jnp.pad(cache_conv, ((0, 0), (0, 1), (hist_pad, 0), (0, 0)))
    ys, wkv_s, shift_s, conv_s = _run_group(xs, list(wkv0), list(shift0), list(conv0), pp,
                                            tm=nb * dt, nb=SMALL_STREAMS, tt=dt, chunk=dt, proj_streams=SMALL_STREAMS)
    yp, wkv_p, shift_p, conv_p = _run_group(
        x_prompt, [s[db:] for s in wkv_s], [s[db:] for s in shift_s], [s[db:] for s in conv_s],
        pp, tm=512, nb=PROMPT_STREAMS, tt=PROMPT_TILE, chunk=WKV_CHUNK, proj_streams=1)

    stack = lambda xs_, f: jnp.stack([f(x) for x in xs_])
    return (yp, ys[:db],
            stack(wkv_p, lambda s: s),
            stack(shift_p, lambda s: s[:, 0, :P_RWKV]),
            stack(conv_p, lambda s: s[:, hist_pad:]),
            stack(wkv_s, lambda s: s[:db]),
            stack(shift_s, lambda s: s[:db, 0, :P_RWKV]),
            stack(conv_s, lambda s: s[:db, hist_pad:]))
```

```python
import functools

import jax
import jax.numpy as jnp
from jax import lax
from jax.experimental import pallas as pl
from jax.experimental.pallas import tpu as pltpu

D_MODEL = 1024
DEPTH = 4
N_META = 16
HEAD_DIM = 64
N_HEADS = 8
D_RWKV = N_HEADS * HEAD_DIM
D_CONV = D_MODEL - D_RWKV
LORA_W = 64
LORA_A = 64
LORA_G = 160
CONV_WIDTH = 31
D_FF = 4 * D_MODEL
P_RWKV = 3 * D_RWKV + LORA_W + LORA_A + LORA_G
NORM_EPS = 1e-6
LN_EPS = 1e-5
GN_EPS = 64e-5
DECAY_SCALE = 0.606531

LANES = 128
P_RWKV_PAD = 15 * LANES
LORA_G_PAD = P_RWKV_PAD - (3 * D_RWKV + LORA_W + LORA_A)
O_WA = 3 * D_RWKV
O_G = O_WA + LORA_W + LORA_A
CONV_HIST = 32
WKV_CHUNK = 64
PROMPT_STREAMS = 4
PROMPT_TILE = 128
SMALL_STREAMS = 11
FF_BLOCK = 1024
VMEM_LIMIT = 48 * 1024 * 1024

ROWS = {name: i for i, name in enumerate((
    "mu", "w0a0", "k_k", "k_a", "r_k", "gn_g", "gn_b", "conv_b", "cln_g", "cln_b",
    "norm_mix", "norm_ffn", "norm_final", "pad0", "pad1", "pad2"))}

F32 = jnp.float32
BF16 = jnp.bfloat16


def _dot(a, b):
    return jnp.dot(a, b, preferred_element_type=F32)


def _dot_nt(a, b):
    return lax.dot_general(a, b, (((1,), (1,)), ((), ())), preferred_element_type=F32)


def _dot_tn(a, b):
    return lax.dot_general(a, b, (((0,), (0,)), ((), ())), preferred_element_type=F32)


def _split(x):
    hi = x.astype(BF16)
    lo = (x - hi.astype(F32)).astype(BF16)
    return hi, lo


def _sigmoid(x):
    return 0.5 * jnp.tanh(0.5 * x) + 0.5


def _rms(x, g):
    return x * lax.rsqrt(jnp.mean(x * x, axis=-1, keepdims=True) + NORM_EPS) * g


def _layer_spec(shape, l):
    return pl.BlockSpec((None,) + shape, lambda *_: (l,) + (0,) * len(shape))


def _head_sum(x):
    i = lax.broadcasted_iota(jnp.int32, (LANES, LANES), 0)
    j = lax.broadcasted_iota(jnp.int32, (LANES, LANES), 1)
    ones = jnp.where(((i ^ j) & HEAD_DIM) == 0, 1.0, 0.0).astype(BF16)
    outs = []
    for g in range(D_RWKV // LANES):
        hi, lo = _split(x[:, g * LANES:(g + 1) * LANES])
        outs.append(_dot(hi, ones) + _dot(lo, ones))
    return jnp.concatenate(outs, axis=1)


def _bd(x):
    low = lax.broadcasted_iota(jnp.int32, x.shape, 1) < x.shape[1] // 2
    return jnp.concatenate([jnp.where(low, x, 0.0), jnp.where(low, 0.0, x)], axis=0)


def _pair_masks(c):
    ti = lax.broadcasted_iota(jnp.int32, (c, 2 * c), 0)
    si = lax.broadcasted_iota(jnp.int32, (c, 2 * c), 1) & (c - 1)
    offs = []
    n = 1
    while n < c:
        same = ((ti ^ si) & ~(2 * n - 1)) == 0
        offs.append(same & ((ti & n) != 0) & ((si & n) == 0))
        n *= 2
    return ti > si, ti >= si, jnp.where(ti == si, 1.0, 0.0).astype(F32), offs


def _zero_token(x):
    tot = None
    for r0 in range(0, x.shape[0], 8):
        for c0 in range(0, x.shape[1], LANES):
            blk = x[r0:r0 + 8, c0:c0 + LANES]
            tot = blk if tot is None else tot + blk
    bits = lax.shift_right_logical(pltpu.bitcast(tot, jnp.uint32), jnp.uint32(32))
    return pltpu.bitcast(bits, F32)


def _pin(xs, sync):
    token = sync.pop("token", None)
    if token is None:
        return xs
    return [jnp.concatenate([xs[0][0:8] + token[:, :xs[0].shape[1]], xs[0][8:]], axis=0)] + xs[1:]


def _wkv_prepare(lhs, bdb, bdk, bdv, masks, sync):
    strict, incl, eye, offs = masks
    n = range(len(lhs))
    c = lhs[0].shape[0] // 2
    gb = [_dot_nt(lhs[i], bdb[i]) for i in n]
    yield
    gk = [_dot_nt(lhs[i], bdk[i]) for i in n]
    yield
    lb = [jnp.where(strict, gb[i][:c], 0.0) for i in n]
    arb = [jnp.where(incl, gb[i][c:], 0.0).astype(BF16) for i in n]
    both = jnp.concatenate([strict, incl], axis=0)
    gkm = _pin([jnp.where(both, gk[i], 0.0) for i in n], sync)
    kv = [_dot(gkm[i].astype(BF16), bdv[i]) for i in n]
    yield
    t = [eye - jnp.where(offs[0], lb[i], 0.0) for i in n]
    for off in offs[1:]:
        t = _pin(t, sync)
        lt = [_dot(jnp.where(off, lb[i], 0.0).astype(BF16), _bd(t[i]).astype(BF16)) for i in n]
        yield
        lt = _pin(lt, sync)
        t = [t[i] - _dot(t[i].astype(BF16), _bd(lt[i]).astype(BF16)) for i in n]
        yield
    return t, kv, arb


def _wkv_advance(s0, lhs, t, kv, arb, uv_v, uv_rhs, wc, diag, sync):
    n = range(len(s0))
    c = lhs[0].shape[0] // 2
    s0p = _pin(list(s0), sync)
    x = [_dot_nt(lhs[i], s0p[i].astype(BF16)) for i in n]
    yield
    th = [_split(t[i]) for i in n]
    rhs = _pin([-x[i][:c] - kv[i][:c] for i in n], sync)
    rh = [_split(_bd(rhs[i])) for i in n]
    uu = [_dot(jnp.concatenate(th[i], axis=0), rh[i][0]) for i in n]
    u = [uu[i][:c] + (uu[i][c:] + _dot(th[i][0], rh[i][1])) for i in n]
    yield
    u = _pin(u, sync)
    o = [x[i][c:] + _dot(arb[i], _bd(u[i]).astype(BF16)) + kv[i][c:] for i in n]
    yield
    upd = [_dot_tn(jnp.concatenate([u[i].astype(BF16), uv_v[i]], axis=0), uv_rhs[i]) for i in n]
    s1 = [(s0[i] + jnp.where(diag, upd[i], 0.0)) * wc[i] for i in n]
    yield
    return o, s1


def _mixer_kernel(h_ref, wkv0_ref, shift0_ref, conv0_ref, tab_ref, wr_ref, wc_ref, wwa_ref, lg_ref, cw_ref,
                  mix_ref, wkv_ref, shift_ref, conv_ref,
                  s_ref, carry_ref, ext_ref, kk_s, rt_s, b_s, k_s, v_s, wc_s, g_s, bonus_s, o_s,
                  *, chunk, shared_init, proj_streams):
    j = pl.program_id(1)
    nb, tt, _ = h_ref.shape
    n_chunks = tt // chunk
    n_pairs = D_RWKV // LANES
    tab = lambda name, width: tab_ref[ROWS[name]:ROWS[name] + 1, 0:width]

    @pl.when(j == 0)
    def _():
        for s in range(nb):
            init = 0 if shared_init else s
            s_ref[s] = jnp.zeros(s_ref.shape[1:], F32)
            for p in range(n_pairs):
                s_ref[s, p, 0:HEAD_DIM, 0:HEAD_DIM] = wkv0_ref[init, 2 * p]
                s_ref[s, p, HEAD_DIM:LANES, HEAD_DIM:LANES] = wkv0_ref[init, 2 * p + 1]
            carry_ref[s, 0:1, :] = shift0_ref[init]
            ext_ref[s, 0:CONV_HIST, :] = conv0_ref[init]

    first = CONV_HIST - (CONV_WIDTH - 1)
    ti = lax.broadcasted_iota(jnp.int32, (tt, tt), 0)
    si = lax.broadcasted_iota(jnp.int32, (tt, tt), 1)
    tril = jnp.where((ti >= si) & ((ti ^ si) < chunk), 1.0, 0.0).astype(BF16)
    for s in range(nb):
        rows = slice(s * tt, (s + 1) * tt)
        if s % proj_streams == 0:
            x = h_ref[s:s + proj_streams].reshape(proj_streams * tt, D_MODEL)
            xn = _rms(x, tab("norm_mix", D_MODEL)).astype(BF16)
            proj_r = _dot(xn, wr_ref[...])
            proj_c = _dot(xn, wc_ref[...])
        mine = slice((s % proj_streams) * tt, (s % proj_streams + 1) * tt)
        p = proj_r[mine]
        u = proj_c[mine, :D_CONV] * _sigmoid(proj_c[mine, D_CONV:])
        row = lax.broadcasted_iota(jnp.int32, p.shape, 0)
        prev = jnp.where(row == 0, carry_ref[s, 0:1, :], pltpu.roll(p, 1, 0))
        xs = p + (prev - p) * tab("mu", P_RWKV_PAD)
        carry_ref[s, 0:1, :] = p[tt - 1:tt, :]

        r = xs[:, 0:D_RWKV]
        k = xs[:, D_RWKV:2 * D_RWKV]
        v = xs[:, 2 * D_RWKV:3 * D_RWKV]
        xwa = xs[:, O_WA:O_G]
        lane = lax.broadcasted_iota(jnp.int32, xwa.shape, 1)
        wa = (_dot(jnp.where(lane < LORA_W, jnp.tanh(xwa), xwa).astype(BF16), wwa_ref[...])
              + tab("w0a0", 2 * D_RWKV))
        logw = -DECAY_SCALE * _sigmoid(wa[:, :D_RWKV])
        a = _sigmoid(wa[:, D_RWKV:])
        g_s[rows, :] = _dot(_sigmoid(xs[:, O_G:]).astype(BF16), lg_ref[...])

        kk = k * tab("k_k", D_RWKV)
        kk = kk / jnp.maximum(jnp.sqrt(_head_sum(kk * kk)), 1e-12)
        k = k * (1.0 + (a - 1.0) * tab("k_a", D_RWKV))
        bonus_s[rows, :] = _head_sum(r * k * tab("r_k", D_RWKV)) * v

        l_hi, l_lo = _split(logw)
        cl = _dot(tril, l_hi) + _dot(tril, l_lo)
        w_inc = jnp.exp(cl)
        w_inv = jnp.exp(-cl)
        kk_s[rows, :] = (kk * jnp.exp(cl - logw)).astype(BF16)
        rt_s[rows, :] = (r * w_inc).astype(BF16)
        b_s[rows, :] = (kk * a * w_inv).astype(BF16)
        k_s[rows, :] = (k * w_inv).astype(BF16)
        v_s[rows, :] = v.astype(BF16)
        for c in range(n_chunks):
            wc_s[(s * n_chunks + c) * 8:(s * n_chunks + c + 1) * 8, :] = jnp.broadcast_to(
                w_inc[(c + 1) * chunk - 1:(c + 1) * chunk, :], (8, D_RWKV))

        ext_ref[s, CONV_HIST:CONV_HIST + tt, :] = u

    sync = {}
    conv_rows = min(tt, CONV_HIST)

    def conv_pieces():
        for s in range(nb):
            for r0 in range(0, tt, conv_rows):
                blk = ext_ref[s, r0:r0 + conv_rows + CONV_HIST, :]
                acc = jnp.zeros((conv_rows, D_CONV), F32) + tab("conv_b", D_CONV)
                for q in range(8):
                    eq = blk if q == 0 else pltpu.roll(blk, conv_rows + CONV_HIST - q, 0)
                    for o in range(first + (q - first) % 8, CONV_HIST + 1, 8):
                        tap = eq[o - q:o - q + conv_rows, :].reshape(conv_rows // 8, 8, D_CONV) * cw_ref[o - first]
                        acc = acc + tap.reshape(conv_rows, D_CONV)
                    sync["token"] = _zero_token(acc)
                    yield
                m = jnp.mean(acc, axis=-1, keepdims=True)
                cc = acc - m
                cv = jnp.mean(cc * cc, axis=-1, keepdims=True)
                ln = cc * lax.rsqrt(cv + LN_EPS) * tab("cln_g", D_CONV) + tab("cln_b", D_CONV)
                mix_ref[s, r0:r0 + conv_rows, D_RWKV:D_MODEL] = ln * _sigmoid(ln)
                yield

    def recurrence():
        masks = _pair_masks(chunk)
        di = lax.broadcasted_iota(jnp.int32, (LANES, LANES), 0)
        dj = lax.broadcasted_iota(jnp.int32, (LANES, LANES), 1)
        diag = ((di ^ dj) & HEAD_DIM) == 0
        chains = [(s, p) for s in range(nb) for p in range(n_pairs)]
        cat = lambda a, b: jnp.concatenate([a, b], axis=0)
        bd16 = lambda x: _bd(x.astype(F32)).astype(BF16)
        prepared = []
        for c in range(n_chunks):
            at = lambda ref, s, p: ref[s * tt + c * chunk:s * tt + (c + 1) * chunk, p * LANES:(p + 1) * LANES]
            lhs = [cat(at(kk_s, s, p), at(rt_s, s, p)) for s, p in chains]
            bdb = [bd16(at(b_s, s, p)) for s, p in chains]
            bdk = [bd16(at(k_s, s, p)) for s, p in chains]
            bdv = [bd16(at(v_s, s, p)) for s, p in chains]
            prepared.append((lhs,) + tuple((yield from _wkv_prepare(lhs, bdb, bdk, bdv, masks, sync))))
        for c in range(n_chunks):
            at = lambda ref, s, p: ref[s * tt + c * chunk:s * tt + (c + 1) * chunk, p * LANES:(p + 1) * LANES]
            lhs, t, kv, arb = prepared[c]
            uv_v = [at(v_s, s, p) for s, p in chains]
            uv_rhs = [cat(at(b_s, s, p), at(k_s, s, p)) for s, p in chains]
            wc = [wc_s[(s * n_chunks + c) * 8:(s * n_chunks + c) * 8 + 1, p * LANES:(p + 1) * LANES]
                  for s, p in chains]
            o, s1 = yield from _wkv_advance([s_ref[s, p] for s, p in chains], lhs, t, kv, arb, uv_v, uv_rhs,
                                            wc, diag, sync)
            for i, (s, p) in enumerate(chains):
                o_s[s * tt + c * chunk:s * tt + (c + 1) * chunk, p * LANES:(p + 1) * LANES] = o[i]
                s_ref[s, p] = s1[i]

    conv = conv_pieces()
    n_conv = nb * (tt // conv_rows) * 9
    n_stages = n_chunks * (3 + 2 * (len(_pair_masks(chunk)[3]) - 1) + 4)
    per_stage = -(-n_conv // n_stages)
    for _ in recurrence():
        for _ in range(per_stage):
            next(conv, None)
    for _ in conv:
        pass
    for s in range(nb):
        ext_ref[s, 0:CONV_HIST, :] = ext_ref[s, tt:tt + CONV_HIST, :]

    streams = range(nb)
    rows = [slice(s * tt, (s + 1) * tt) for s in streams]
    mean = [_head_sum(o_s[rows[s], :]) * (1.0 / HEAD_DIM) for s in streams]
    oc = [o_s[rows[s], :] - mean[s] for s in streams]
    var = [_head_sum(oc[s] * oc[s]) * (1.0 / HEAD_DIM) for s in streams]
    for s in streams:
        mix_ref[s, :, 0:D_RWKV] = (oc[s] * lax.rsqrt(var[s] + GN_EPS) * tab("gn_g", D_RWKV) + tab("gn_b", D_RWKV)
                                   + bonus_s[rows[s], :]) * g_s[rows[s], :]

    @pl.when(j == pl.num_programs(1) - 1)
    def _():
        for s in range(nb):
            shift_ref[s] = carry_ref[s, 0:1, :]
            conv_ref[s] = ext_ref[s, 0:CONV_HIST, :]
            for p in range(n_pairs):
                wkv_ref[s, 2 * p] = s_ref[s, p, 0:HEAD_DIM, 0:HEAD_DIM]
                wkv_ref[s, 2 * p + 1] = s_ref[s, p, HEAD_DIM:LANES, HEAD_DIM:LANES]


def _mixer(h, wkv0, shift0, conv0, pp, l, nb, tt, chunk, proj_streams):
    b, t, _ = h.shape
    shared = wkv0.shape[0] != b
    n_pairs = D_RWKV // LANES

    def st(*tail):
        if shared:
            return pl.BlockSpec((1,) + tail, lambda i, j: (0,) * (1 + len(tail)))
        return pl.BlockSpec((nb,) + tail, lambda i, j: (i,) + (0,) * len(tail))

    rows = nb * tt
    bf = pltpu.VMEM((rows, D_RWKV), BF16)
    f32 = pltpu.VMEM((rows, D_RWKV), F32)
    kern = functools.partial(_mixer_kernel, chunk=chunk, shared_init=shared, proj_streams=proj_streams)
    return pl.pallas_call(
        kern,
        grid=(b // nb, t // tt),
        in_specs=[pl.BlockSpec((nb, tt, D_MODEL), lambda i, j: (i, j, 0)),
                  st(N_HEADS, HEAD_DIM, HEAD_DIM), st(1, P_RWKV_PAD), st(CONV_HIST, D_CONV),
                  _layer_spec((len(ROWS), P_RWKV_PAD), l),
                  _layer_spec((D_MODEL, P_RWKV_PAD), l),
                  _layer_spec((D_MODEL, 2 * D_CONV), l),
                  _layer_spec((LORA_W + LORA_A, 2 * D_RWKV), l),
                  _layer_spec((LORA_G_PAD, D_RWKV), l),
                  _layer_spec((CONV_HIST, 8, D_CONV), l)],
        out_specs=[pl.BlockSpec((nb, tt, D_MODEL), lambda i, j: (i, j, 0)),
                   pl.BlockSpec((nb, N_HEADS, HEAD_DIM, HEAD_DIM), lambda i, j: (i, 0, 0, 0)),
                   pl.BlockSpec((nb, 1, P_RWKV_PAD), lambda i, j: (i, 0, 0)),
                   pl.BlockSpec((nb, CONV_HIST, D_CONV), lambda i, j: (i, 0, 0))],
        out_shape=[jax.ShapeDtypeStruct((b, t, D_MODEL), F32),
                   jax.ShapeDtypeStruct((b, N_HEADS, HEAD_DIM, HEAD_DIM), F32),
                   jax.ShapeDtypeStruct((b, 1, P_RWKV_PAD), F32),
                   jax.ShapeDtypeStruct((b, CONV_HIST, D_CONV), F32)],
        scratch_shapes=[pltpu.VMEM((nb, n_pairs, LANES, LANES), F32),
                        pltpu.VMEM((nb, 8, P_RWKV_PAD), F32),
                        pltpu.VMEM((nb, CONV_HIST + tt, D_CONV), F32),
                        bf, bf, bf, bf, bf,
                        pltpu.VMEM((8 * nb * (tt // chunk), D_RWKV), F32),
                        f32, f32, f32],
        compiler_params=pltpu.CompilerParams(dimension_semantics=("parallel", "arbitrary"),
                                             vmem_limit_bytes=VMEM_LIMIT),
        name="mixer",
    )(h, wkv0, shift0, conv0, pp["table"], pp["w_r"], pp["w_c"], pp["wwa"], pp["lg"], pp["conv_w"])


def _ffn_kernel(h_ref, mix_ref, tab_ref, wo_ref, wu_ref, wd_ref, out_ref, *, final):
    tab = lambda name: tab_ref[ROWS[name]:ROWS[name] + 1, 0:D_MODEL]
    h1 = h_ref[...] + _dot(mix_ref[...].astype(BF16), wo_ref[...])
    hn = _rms(h1, tab("norm_ffn")).astype(BF16)
    acc = h1
    for c in range(D_FF // FF_BLOCK):
        cols = slice(c * FF_BLOCK, (c + 1) * FF_BLOCK)
        up = jnp.maximum(_dot(hn, wu_ref[:, cols]), 0.0)
        acc = acc + _dot((up * up).astype(BF16), wd_ref[cols, :])
    out_ref[...] = _rms(acc, tab("norm_final")) if final else acc


def _ffn(h, mix, pp, l, tm, final):
    n = h.shape[0]
    tile = pl.BlockSpec((tm, D_MODEL), lambda i: (i, 0))
    return pl.pallas_call(
        functools.partial(_ffn_kernel, final=final),
        grid=(n // tm,),
        in_specs=[tile, tile,
                  _layer_spec((len(ROWS), P_RWKV_PAD), l),
                  _layer_spec((D_MODEL, D_MODEL), l),
                  _layer_spec((D_MODEL, D_FF), l),
                  _layer_spec((D_FF, D_MODEL), l)],
        out_specs=tile,
        out_shape=jax.ShapeDtypeStruct((n, D_MODEL), F32),
        compiler_params=pltpu.CompilerParams(dimension_semantics=("parallel",),
                                             vmem_limit_bytes=VMEM_LIMIT),
        name="ffn",
    )(h, mix, pp["table"], pp["w_out"], pp["w_up"], pp["w_down"])


def _prepare_params(norm_mix, w_in, mu_shift, w0, lora_w, a0, lora_a, lora_g, k_k, k_a, r_k, gn_g, gn_b,
                    conv_w, conv_b, cln_g, cln_b, w_out, norm_ffn, w_up, w_down, norm_final):
    depth = w_in.shape[0]
    pad_last = lambda x, n: jnp.pad(x, [(0, 0)] * (x.ndim - 1) + [(0, n - x.shape[-1])])
    vec = lambda x: pad_last(x.reshape(depth, -1).astype(F32), P_RWKV_PAD)
    rows = dict(mu=mu_shift, w0a0=jnp.concatenate([w0, a0], axis=-1), k_k=k_k, k_a=k_a, r_k=r_k, gn_g=gn_g,
                gn_b=gn_b, conv_b=conv_b, cln_g=cln_g, cln_b=cln_b, norm_mix=norm_mix, norm_ffn=norm_ffn,
                norm_final=jnp.broadcast_to(norm_final, (depth, D_MODEL)))
    zero = jnp.zeros((depth, P_RWKV_PAD), F32)
    table = jnp.stack([vec(rows[name]) if name in rows else zero for name in ROWS], axis=1)
    w_in = w_in.astype(BF16)
    w_r = pad_last(w_in[..., :P_RWKV], P_RWKV_PAD)
    w_c = w_in[..., P_RWKV:]
    zeros = jnp.zeros_like(lora_w)
    wwa = jnp.concatenate([jnp.concatenate([lora_w, zeros], axis=-1),
                           jnp.concatenate([zeros, lora_a], axis=-1)], axis=-2).astype(BF16)
    return dict(
        table=table, w_r=w_r, w_c=w_c, wwa=wwa,
        lg=jnp.pad(lora_g, ((0, 0), (0, LORA_G_PAD - LORA_G), (0, 0))).astype(BF16),
        conv_w=jnp.broadcast_to(jnp.pad(conv_w, ((0, 0), (0, CONV_HIST - CONV_WIDTH), (0, 0)))[:, :, None, :],
                                (depth, CONV_HIST, 8, D_CONV)),
        w_out=w_out.astype(BF16), w_up=w_up.astype(BF16), w_down=w_down.astype(BF16))


def _run_group(x, wkv0, shift0, conv0, pp, tm, nb, tt, chunk, proj_streams):
    b, t, _ = x.shape
    h = x
    wkv, shift, conv = [], [], []
    for l in range(DEPTH):
        mix, s1, s2, s3 = _mixer(h, wkv0[l], shift0[l], conv0[l], pp, l, nb, tt, chunk, proj_streams)
        h = _ffn(h.reshape(b * t, D_MODEL), mix.reshape(b * t, D_MODEL), pp, l, tm,
                 final=(l == DEPTH - 1)).reshape(b, t, D_MODEL)
        wkv.append(s1)
        shift.append(s2)
        conv.append(s3)
    return h, wkv, shift, conv


def kernel(x_prompt, x_sample, state_wkv, state_shift, cache_conv, meta_tokens, norm_mix, w_in, mu_shift, w0, lora_w, a0, lora_a, lora_g, k_k, k_a, r_k, gn_g, gn_b, conv_w, conv_b, cln_g, cln_b, w_out, norm_ffn, w_up, w_down, norm_final):
    pp = _prepare_params(norm_mix, w_in, mu_shift, w0, lora_w, a0, lora_a, lora_g, k_k, k_a, r_k, gn_g, gn_b,
                         conv_w, conv_b, cln_g, cln_b, w_out, norm_ffn, w_up, w_down, norm_final)
    db, dt, _ = x_sample.shape
    hist_pad = CONV_HIST - (CONV_WIDTH - 1)

    xs = jnp.concatenate([x_sample, meta_tokens[None].astype(x_sample.dtype)], axis=0)
    nb = db + 1
    wkv0 = jnp.pad(state_wkv, ((0, 0), (0, 1), (0, 0), (0, 0), (0, 0)))
    shift0 = jnp.pad(state_shift, ((0, 0), (0, 1), (0, P_RWKV_PAD - P_RWKV)))[:, :, None, :]
    conv0 = jnp.pad(cache_conv, ((0, 0), (0, 1), (hist_pad, 0), (0, 0)))
    ys, wkv_s, shift_s, conv_s = _run_group(xs, list(wkv0), list(shift0), list(conv0), pp,
                                            tm=nb * dt, nb=SMALL_STREAMS, tt=dt, chunk=dt, proj_streams=SMALL_STREAMS)
    yp, wkv_p, shift_p, conv_p = _run_group(
        x_prompt, [s[db:] for s in wkv_s], [s[db:] for s in shift_s], [s[db:] for s in conv_s],
        pp, tm=512, nb=PROMPT_STREAMS, tt=PROMPT_TILE, chunk=WKV_CHUNK, proj_streams=1)

    stack = lambda xs_, f: jnp.stack([f(x) for x in xs_])
    return (yp, ys[:db],
            stack(wkv_p, lambda s: s),
            stack(shift_p, lambda s: s[:, 0, :P_RWKV]),
            stack(conv_p, lambda s: s[:, hist_pad:]),
            stack(wkv_s, lambda s: s[:db]),
            stack(shift_s, lambda s: s[:db, 0, :P_RWKV]),
            stack(conv_s, lambda s: s[:db, hist_pad:]))
```

```python
import functools

import jax
import jax.numpy as jnp
from jax import lax
from jax.experimental import pallas as pl
from jax.experimental.pallas import tpu as pltpu

D_MODEL = 1024
DEPTH = 4
N_META = 16
HEAD_DIM = 64
N_HEADS = 8
D_RWKV = N_HEADS * HEAD_DIM
D_CONV = D_MODEL - D_RWKV
LORA_W = 64
LORA_A = 64
LORA_G = 160
CONV_WIDTH = 31
D_FF = 4 * D_MODEL
P_RWKV = 3 * D_RWKV + LORA_W + LORA_A + LORA_G
NORM_EPS = 1e-6
LN_EPS = 1e-5
GN_EPS = 64e-5
DECAY_SCALE = 0.606531

LANES = 128
P_RWKV_PAD = 15 * LANES
LORA_G_PAD = P_RWKV_PAD - (3 * D_RWKV + LORA_W + LORA_A)
O_WA = 3 * D_RWKV
O_G = O_WA + LORA_W + LORA_A
CONV_HIST = 32
WKV_CHUNK = 64
PROMPT_STREAMS = 4
PROMPT_TILE = 128
SMALL_STREAMS = 11
FF_BLOCK = 1024
VMEM_LIMIT = 48 * 1024 * 1024

ROWS = {name: i for i, name in enumerate((
    "mu", "w0a0", "k_k", "k_a", "r_k", "gn_g", "gn_b", "conv_b", "cln_g", "cln_b",
    "norm_mix", "norm_ffn", "norm_final", "pad0", "pad1", "pad2"))}

F32 = jnp.float32
BF16 = jnp.bfloat16


def _dot(a, b):
    return jnp.dot(a, b, preferred_element_type=F32)


def _dot_nt(a, b):
    return lax.dot_general(a, b, (((1,), (1,)), ((), ())), preferred_element_type=F32)


def _dot_tn(a, b):
    return lax.dot_general(a, b, (((0,), (0,)), ((), ())), preferred_element_type=F32)


def _split(x):
    hi = x.astype(BF16)
    lo = (x - hi.astype(F32)).astype(BF16)
    return hi, lo


def _sigmoid(x):
    return 0.5 * jnp.tanh(0.5 * x) + 0.5


def _rms(x, g):
    return x * lax.rsqrt(jnp.mean(x * x, axis=-1, keepdims=True) + NORM_EPS) * g


def _layer_spec(shape, l):
    return pl.BlockSpec((None,) + shape, lambda *_: (l,) + (0,) * len(shape))


def _head_sum(x):
    i = lax.broadcasted_iota(jnp.int32, (LANES, LANES), 0)
    j = lax.broadcasted_iota(jnp.int32, (LANES, LANES), 1)
    ones = jnp.where(((i ^ j) & HEAD_DIM) == 0, 1.0, 0.0).astype(BF16)
    outs = []
    for g in range(D_RWKV // LANES):
        hi, lo = _split(x[:, g * LANES:(g + 1) * LANES])
        outs.append(_dot(hi, ones) + _dot(lo, ones))
    return jnp.concatenate(outs, axis=1)


def _bd(x):
    low = lax.broadcasted_iota(jnp.int32, x.shape, 1) < x.shape[1] // 2
    return jnp.concatenate([jnp.where(low, x, 0.0), jnp.where(low, 0.0, x)], axis=0)


def _pair_masks(c):
    ti = lax.broadcasted_iota(jnp.int32, (c, 2 * c), 0)
    si = lax.broadcasted_iota(jnp.int32, (c, 2 * c), 1) & (c - 1)
    offs = []
    n = 1
    while n < c:
        same = ((ti ^ si) & ~(2 * n - 1)) == 0
        offs.append(same & ((ti & n) != 0) & ((si & n) == 0))
        n *= 2
    return ti > si, ti >= si, jnp.where(ti == si, 1.0, 0.0).astype(F32), offs


def _zero_token(x):
    tot = None
    for r0 in range(0, x.shape[0], 8):
        for c0 in range(0, x.shape[1], LANES):
            blk = x[r0:r0 + 8, c0:c0 + LANES]
            tot = blk if tot is None else tot + blk
    bits = lax.shift_right_logical(pltpu.bitcast(tot, jnp.uint32), jnp.uint32(32))
    return pltpu.bitcast(bits, F32)


def _pin(xs, sync):
    token = sync.pop("token", None)
    if token is None:
        return xs
    return [jnp.concatenate([xs[0][0:8] + token[:, :xs[0].shape[1]], xs[0][8:]], axis=0)] + xs[1:]


def _wkv_prepare(lhs, bdb, bdk, bdv, masks, sync):
    strict, incl, eye, offs = masks
    n = range(len(lhs))
    c = lhs[0].shape[0] // 2
    gb = [_dot_nt(lhs[i], bdb[i]) for i in n]
    yield
    gk = [_dot_nt(lhs[i], bdk[i]) for i in n]
    yield
    lb = [jnp.where(strict, gb[i][:c], 0.0) for i in n]
    arb = [jnp.where(incl, gb[i][c:], 0.0).astype(BF16) for i in n]
    both = jnp.concatenate([strict, incl], axis=0)
    gkm = _pin([jnp.where(both, gk[i], 0.0) for i in n], sync)
    kv = [_dot(gkm[i].astype(BF16), bdv[i]) for i in n]
    yield
    t = [eye - jnp.where(offs[0], lb[i], 0.0) for i in n]
    for off in offs[1:]:
        t = _pin(t, sync)
        lt = [_dot(jnp.where(off, lb[i], 0.0).astype(BF16), _bd(t[i]).astype(BF16)) for i in n]
        yield
        lt = _pin(lt, sync)
        t = [t[i] - _dot(t[i].astype(BF16), _bd(lt[i]).astype(BF16)) for i in n]
        yield
    return t, kv, arb


def _wkv_advance(s0, lhs, t, kv, arb, uv_v, uv_rhs, wc, diag, sync):
    n = range(len(s0))
    c = lhs[0].shape[0] // 2
    s0p = _pin(list(s0), sync)
    x = [_dot_nt(lhs[i], s0p[i].astype(BF16)) for i in n]
    yield
    th = [_split(t[i]) for i in n]
    rhs = _pin([-x[i][:c] - kv[i][:c] for i in n], sync)
    rh = [_split(_bd(rhs[i])) for i in n]
    uu = [_dot(jnp.concatenate(th[i], axis=0), rh[i][0]) for i in n]
    u = [uu[i][:c] + (uu[i][c:] + _dot(th[i][0], rh[i][1])) for i in n]
    yield
    u = _pin(u, sync)
    o = [x[i][c:] + _dot(arb[i], _bd(u[i]).astype(BF16)) + kv[i][c:] for i in n]
    yield
    upd = [_dot_tn(jnp.concatenate([u[i].astype(BF16), uv_v[i]], axis=0), uv_rhs[i]) for i in n]
    s1 = [(s0[i] + jnp.where(diag, upd[i], 0.0)) * wc[i] for i in n]
    yield
    return o, s1


def _mixer_kernel(h_ref, wkv0_ref, shift0_ref, conv0_ref, tab_ref, wr_ref, wc_ref, wwa_ref, lg_ref, cw_ref,
                  mix_ref, wkv_ref, shift_ref, conv_ref,
                  s_ref, carry_ref, ext_ref, kk_s, rt_s, b_s, k_s, v_s, wc_s, g_s, bonus_s, o_s,
                  *, chunk, shared_init, proj_streams):
    j = pl.program_id(1)
    nb, tt, _ = h_ref.shape
    n_chunks = tt // chunk
    n_pairs = D_RWKV // LANES
    tab = lambda name, width: tab_ref[ROWS[name]:ROWS[name] + 1, 0:width]

    @pl.when(j == 0)
    def _():
        for s in range(nb):
            init = 0 if shared_init else s
            s_ref[s] = jnp.zeros(s_ref.shape[1:], F32)
            for p in range(n_pairs):
                s_ref[s, p, 0:HEAD_DIM, 0:HEAD_DIM] = wkv0_ref[init, 2 * p]
                s_ref[s, p, HEAD_DIM:LANES, HEAD_DIM:LANES] = wkv0_ref[init, 2 * p + 1]
            carry_ref[s, 0:1, :] = shift0_ref[init]
            ext_ref[s, 0:CONV_HIST, :] = conv0_ref[init]

    first = CONV_HIST - (CONV_WIDTH - 1)
    ti = lax.broadcasted_iota(jnp.int32, (tt, tt), 0)
    si = lax.broadcasted_iota(jnp.int32, (tt, tt), 1)
    tril = jnp.where((ti >= si) & ((ti ^ si) < chunk), 1.0, 0.0).astype(BF16)
    for s in range(nb):
        rows = slice(s * tt, (s + 1) * tt)
        if s % proj_streams == 0:
            x = h_ref[s:s + proj_streams].reshape(proj_streams * tt, D_MODEL)
            xn = _rms(x, tab("norm_mix", D_MODEL)).astype(BF16)
            proj_r = _dot(xn, wr_ref[...])
            proj_c = _dot(xn, wc_ref[...])
        mine = slice((s % proj_streams) * tt, (s % proj_streams + 1) * tt)
        p = proj_r[mine]
        u = proj_c[mine, :D_CONV] * _sigmoid(proj_c[mine, D_CONV:])
        row = lax.broadcasted_iota(jnp.int32, p.shape, 0)
        prev = jnp.where(row == 0, carry_ref[s, 0:1, :], pltpu.roll(p, 1, 0))
        xs = p + (prev - p) * tab("mu", P_RWKV_PAD)
        carry_ref[s, 0:1, :] = p[tt - 1:tt, :]

        r = xs[:, 0:D_RWKV]
        k = xs[:, D_RWKV:2 * D_RWKV]
        v = xs[:, 2 * D_RWKV:3 * D_RWKV]
        xwa = xs[:, O_WA:O_G]
        lane = lax.broadcasted_iota(jnp.int32, xwa.shape, 1)
        wa = (_dot(jnp.where(lane < LORA_W, jnp.tanh(xwa), xwa).astype(BF16), wwa_ref[...])
              + tab("w0a0", 2 * D_RWKV))
        logw = -DECAY_SCALE * _sigmoid(wa[:, :D_RWKV])
        a = _sigmoid(wa[:, D_RWKV:])
        g_s[rows, :] = _dot(_sigmoid(xs[:, O_G:]).astype(BF16), lg_ref[...])

        kk = k * tab("k_k", D_RWKV)
        kk = kk / jnp.maximum(jnp.sqrt(_head_sum(kk * kk)), 1e-12)
        k = k * (1.0 + (a - 1.0) * tab("k_a", D_RWKV))
        bonus_s[rows, :] = _head_sum(r * k * tab("r_k", D_RWKV)) * v

        l_hi, l_lo = _split(logw)
        cl = _dot(tril, l_hi) + _dot(tril, l_lo)
        w_inc = jnp.exp(cl)
        w_inv = jnp.exp(-cl)
        kk_s[rows, :] = (kk * jnp.exp(cl - logw)).astype(BF16)
        rt_s[rows, :] = (r * w_inc).astype(BF16)
        b_s[rows, :] = (kk * a * w_inv).astype(BF16)
        k_s[rows, :] = (k * w_inv).astype(BF16)
        v_s[rows, :] = v.astype(BF16)
        for c in range(n_chunks):
            wc_s[(s * n_chunks + c) * 8:(s * n_chunks + c + 1) * 8, :] = jnp.broadcast_to(
                w_inc[(c + 1) * chunk - 1:(c + 1) * chunk, :], (8, D_RWKV))

        ext_ref[s, CONV_HIST:CONV_HIST + tt, :] = u

    sync = {}
    conv_rows = min(tt, CONV_HIST)

    def conv_pieces():
        for s in range(nb):
            for r0 in range(0, tt, conv_rows):
                blk = ext_ref[s, r0:r0 + conv_rows + CONV_HIST, :]
                acc = jnp.zeros((conv_rows, D_CONV), F32) + tab("conv_b", D_CONV)
                for q in range(8):
                    eq = blk if q == 0 else pltpu.roll(blk, conv_rows + CONV_HIST - q, 0)
                    for o in range(first + (q - first) % 8, CONV_HIST + 1, 8):
                        tap = eq[o - q:o - q + conv_rows, :].reshape(conv_rows // 8, 8, D_CONV) * cw_ref[o - first]
                        acc = acc + tap.reshape(conv_rows, D_CONV)
                    sync["token"] = _zero_token(acc)
                    yield
                m = jnp.mean(acc, axis=-1, keepdims=True)
                cc = acc - m
                cv = jnp.mean(cc * cc, axis=-1, keepdims=True)
                ln = cc * lax.rsqrt(cv + LN_EPS) * tab("cln_g", D_CONV) + tab("cln_b", D_CONV)
                mix_ref[s, r0:r0 + conv_rows, D_RWKV:D_MODEL] = ln * _sigmoid(ln)
                yield

    def recurrence():
        masks = _pair_masks(chunk)
        di = lax.broadcasted_iota(jnp.int32, (LANES, LANES), 0)
        dj = lax.broadcasted_iota(jnp.int32, (LANES, LANES), 1)
        diag = ((di ^ dj) & HEAD_DIM) == 0
        chains = [(s, p) for s in range(nb) for p in range(n_pairs)]
        cat = lambda a, b: jnp.concatenate([a, b], axis=0)
        bd16 = lambda x: _bd(x.astype(F32)).astype(BF16)
        prepared = []
        for c in range(n_chunks):
            at = lambda ref, s, p: ref[s * tt + c * chunk:s * tt + (c + 1) * chunk, p * LANES:(p + 1) * LANES]
            lhs = [cat(at(kk_s, s, p), at(rt_s, s, p)) for s, p in chains]
            bdb = [bd16(at(b_s, s, p)) for s, p in chains]
            bdk = [bd16(at(k_s, s, p)) for s, p in chains]
            bdv = [bd16(at(v_s, s, p)) for s, p in chains]
            prepared.append((lhs,) + tuple((yield from _wkv_prepare(lhs, bdb, bdk, bdv, masks, sync))))
        for c in range(n_chunks):
            at = lambda ref, s, p: ref[s * tt + c * chunk:s * tt + (c + 1) * chunk, p * LANES:(p + 1) * LANES]
            lhs, t, kv, arb = prepared[c]
            uv_v = [at(v_s, s, p) for s, p in chains]
            uv_rhs = [cat(at(b_s, s, p), at(k_s, s, p)) for s, p in chains]
            wc = [wc_s[(s * n_chunks + c) * 8:(s * n_chunks + c) * 8 + 1, p * LANES:(p + 1) * LANES]
                  for s, p in chains]
            o, s1 = yield from _wkv_advance([s_ref[s, p] for s, p in chains], lhs, t, kv, arb, uv_v, uv_rhs,
                                            wc, diag, sync)
            for i, (s, p) in enumerate(chains):
                o_s[s * tt + c * chunk:s * tt + (c + 1) * chunk, p * LANES:(p + 1) * LANES] = o[i]
                s_ref[s, p] = s1[i]

    conv = conv_pieces()
    n_conv = nb * (tt // conv_rows) * 9
    n_stages = n_chunks * (3 + 2 * (len(_pair_masks(chunk)[3]) - 1) + 4)
    per_stage = -(-n_conv // n_stages)
    for _ in recurrence():
        for _ in range(per_stage):
            next(conv, None)
    for _ in conv:
        pass
    for s in range(nb):
        ext_ref[s, 0:CONV_HIST, :] = ext_ref[s, tt:tt + CONV_HIST, :]

    streams = range(nb)
    rows = [slice(s * tt, (s + 1) * tt) for s in streams]
    mean = [_head_sum(o_s[rows[s], :]) * (1.0 / HEAD_DIM) for s in streams]
    oc = [o_s[rows[s], :] - mean[s] for s in streams]
    var = [_head_sum(oc[s] * oc[s]) * (1.0 / HEAD_DIM) for s in streams]
    for s in streams:
        mix_ref[s, :, 0:D_RWKV] = (oc[s] * lax.rsqrt(var[s] + GN_EPS) * tab("gn_g", D_RWKV) + tab("gn_b", D_RWKV)
                                   + bonus_s[rows[s], :]) * g_s[rows[s], :]

    @pl.when(j == pl.num_programs(1) - 1)
    def _():
        for s in range(nb):
            shift_ref[s] = carry_ref[s, 0:1, :]
            conv_ref[s] = ext_ref[s, 0:CONV_HIST, :]
            for p in range(n_pairs):
                wkv_ref[s, 2 * p] = s_ref[s, p, 0:HEAD_DIM, 0:HEAD_DIM]
                wkv_ref[s, 2 * p + 1] = s_ref[s, p, HEAD_DIM:LANES, HEAD_DIM:LANES]


def _mixer(h, wkv0, shift0, conv0, pp, l, nb, tt, chunk, proj_streams):
    b, t, _ = h.shape
    shared = wkv0.shape[0] != b
    n_pairs = D_RWKV // LANES

    def st(*tail):
        if shared:
            return pl.BlockSpec((1,) + tail, lambda i, j: (0,) * (1 + len(tail)))
        return pl.BlockSpec((nb,) + tail, lambda i, j: (i,) + (0,) * len(tail))

    rows = nb * tt
    bf = pltpu.VMEM((rows, D_RWKV), BF16)
    f32 = pltpu.VMEM((rows, D_RWKV), F32)
    kern = functools.partial(_mixer_kernel, chunk=chunk, shared_init=shared, proj_streams=proj_streams)
    return pl.pallas_call(
        kern,
        grid=(b // nb, t // tt),
        in_specs=[pl.BlockSpec((nb, tt, D_MODEL), lambda i, j: (i, j, 0)),
                  st(N_HEADS, HEAD_DIM, HEAD_DIM), st(1, P_RWKV_PAD), st(CONV_HIST, D_CONV),
                  _layer_spec((len(ROWS), P_RWKV_PAD), l),
                  _layer_spec((D_MODEL, P_RWKV_PAD), l),
                  _layer_spec((D_MODEL, 2 * D_CONV), l),
                  _layer_spec((LORA_W + LORA_A, 2 * D_RWKV), l),
                  _layer_spec((LORA_G_PAD, D_RWKV), l),
                  _layer_spec((CONV_HIST, 8, D_CONV), l)],
        out_specs=[pl.BlockSpec((nb, tt, D_MODEL), lambda i, j: (i, j, 0)),
                   pl.BlockSpec((nb, N_HEADS, HEAD_DIM, HEAD_DIM), lambda i, j: (i, 0, 0, 0)),
                   pl.BlockSpec((nb, 1, P_RWKV_PAD), lambda i, j: (i, 0, 0)),
                   pl.BlockSpec((nb, CONV_HIST, D_CONV), lambda i, j: (i, 0, 0))],
        out_shape=[jax.ShapeDtypeStruct((b, t, D_MODEL), F32),
                   jax.ShapeDtypeStruct((b, N_HEADS, HEAD_DIM, HEAD_DIM), F32),
                   jax.ShapeDtypeStruct((b, 1, P_RWKV_PAD), F32),
                   jax.ShapeDtypeStruct((b, CONV_HIST, D_CONV), F32)],
        scratch_shapes=[pltpu.VMEM((nb, n_pairs, LANES, LANES), F32),
                        pltpu.VMEM((nb, 8, P_RWKV_PAD), F32),
                        pltpu.VMEM((nb, CONV_HIST + tt, D_CONV), F32),
                        bf, bf, bf, bf, bf,
                        pltpu.VMEM((8 * nb * (tt // chunk), D_RWKV), F32),
                        f32, f32, f32],
        compiler_params=pltpu.CompilerParams(dimension_semantics=("parallel", "arbitrary"),
                                             vmem_limit_bytes=VMEM_LIMIT),
        name="mixer",
    )(h, wkv0, shift0, conv0, pp["table"], pp["w_r"], pp["w_c"], pp["wwa"], pp["lg"], pp["conv_w"])


def _ffn_kernel(h_ref, mix_ref, tab_ref, wo_ref, wu_ref, wd_ref, out_ref, *, final):
    tab = lambda name: tab_ref[ROWS[name]:ROWS[name] + 1, 0:D_MODEL]
    h1 = h_ref[...] + _dot(mix_ref[...].astype(BF16), wo_ref[...])
    hn = _rms(h1, tab("norm_ffn")).astype(BF16)
    acc = h1
    for c in range(D_FF // FF_BLOCK):
        cols = slice(c * FF_BLOCK, (c + 1) * FF_BLOCK)
        up = jnp.maximum(_dot(hn, wu_ref[:, cols]), 0.0)
        acc = acc + _dot((up * up).astype(BF16), wd_ref[cols, :])
    out_ref[...] = _rms(acc, tab("norm_final")) if final else acc


def _ffn(h, mix, pp, l, tm, final):
    n = h.shape[0]
    tile = pl.BlockSpec((tm, D_MODEL), lambda i: (i, 0))
    return pl.pallas_call(
        functools.partial(_ffn_kernel, final=final),
        grid=(n // tm,),
        in_specs=[tile, tile,
                  _layer_spec((len(ROWS), P_RWKV_PAD), l),
                  _layer_spec((D_MODEL, D_MODEL), l),
                  _layer_spec((D_MODEL, D_FF), l),
                  _layer_spec((D_FF, D_MODEL), l)],
        out_specs=tile,
        out_shape=jax.ShapeDtypeStruct((n, D_MODEL), F32),
        compiler_params=pltpu.CompilerParams(dimension_semantics=("parallel",),
                                             vmem_limit_bytes=VMEM_LIMIT),
        name="ffn",
    )(h, mix, pp["table"], pp["w_out"], pp["w_up"], pp["w_down"])


def _prepare_params(norm_mix, w_in, mu_shift, w0, lora_w, a0, lora_a, lora_g, k_k, k_a, r_k, gn_g, gn_b,
                    conv_w, conv_b, cln_g, cln_b, w_out, norm_ffn, w_up, w_down, norm_final):
    depth = w_in.shape[0]
    pad_last = lambda x, n: jnp.pad(x, [(0, 0)] * (x.ndim - 1) + [(0, n - x.shape[-1])])
    vec = lambda x: pad_last(x.reshape(depth, -1).astype(F32), P_RWKV_PAD)
    rows = dict(mu=mu_shift, w0a0=jnp.concatenate([w0, a0], axis=-1), k_k=k_k, k_a=k_a, r_k=r_k, gn_g=gn_g,
                gn_b=gn_b, conv_b=conv_b, cln_g=cln_g, cln_b=cln_b, norm_mix=norm_mix, norm_ffn=norm_ffn,
                norm_final=jnp.broadcast_to(norm_final, (depth, D_MODEL)))
    zero = jnp.zeros((depth, P_RWKV_PAD), F32)
    table = jnp.stack([vec(rows[name]) if name in rows else zero for name in ROWS], axis=1)
    w_in = w_in.astype(BF16)
    w_r = pad_last(w_in[..., :P_RWKV], P_RWKV_PAD)
    w_c = w_in[..., P_RWKV:]
    zeros = jnp.zeros_like(lora_w)
    wwa = jnp.concatenate([jnp.concatenate([lora_w, zeros], axis=-1),
                           jnp.concatenate([zeros, lora_a], axis=-1)], axis=-2).astype(BF16)
    return dict(
        table=table, w_r=w_r, w_c=w_c, wwa=wwa,
        lg=jnp.pad(lora_g, ((0, 0), (0, LORA_G_PAD - LORA_G), (0, 0))).astype(BF16),
        conv_w=jnp.broadcast_to(jnp.pad(conv_w, ((0, 0), (0, CONV_HIST - CONV_WIDTH), (0, 0)))[:, :, None, :],
                                (depth, CONV_HIST, 8, D_CONV)),
        w_out=w_out.astype(BF16), w_up=w_up.astype(BF16), w_down=w_down.astype(BF16))


def _run_group(x, wkv0, shift0, conv0, pp, tm, nb, tt, chunk, proj_streams):
    b, t, _ = x.shape
    h = x
    wkv, shift, conv = [], [], []
    for l in range(DEPTH):
        mix, s1, s2, s3 = _mixer(h, wkv0[l], shift0[l], conv0[l], pp, l, nb, tt, chunk, proj_streams)
        h = _ffn(h.reshape(b * t, D_MODEL), mix.reshape(b * t, D_MODEL), pp, l, tm,
                 final=(l == DEPTH - 1)).reshape(b, t, D_MODEL)
        wkv.append(s1)
        shift.append(s2)
        conv.append(s3)
    return h, wkv, shift, conv


def kernel(x_prompt, x_sample, state_wkv, state_shift, cache_conv, meta_tokens, norm_mix, w_in, mu_shift, w0, lora_w, a0, lora_a, lora_g, k_k, k_a, r_k, gn_g, gn_b, conv_w, conv_b, cln_g, cln_b, w_out, norm_ffn, w_up, w_down, norm_final):
    pp = _prepare_params(norm_mix, w_in, mu_shift, w0, lora_w, a0, lora_a, lora_g, k_k, k_a, r_k, gn_g, gn_b,
                         conv_w, conv_b, cln_g, cln_b, w_out, norm_ffn, w_up, w_down, norm_final)
    db, dt, _ = x_sample.shape
    hist_pad = CONV_HIST - (CONV_WIDTH - 1)

    xs = jnp.concatenate([x_sample, meta_tokens[None].astype(x_sample.dtype)], axis=0)
    nb = db + 1
    wkv0 = jnp.pad(state_wkv, ((0, 0), (0, 1), (0, 0), (0, 0), (0, 0)))
    shift0 = jnp.pad(state_shift, ((0, 0), (0, 1), (0, P_RWKV_PAD - P_RWKV)))[:, :, None, :]
    conv0 = jnp.pad(cache_conv, ((0, 0), (0, 1), (hist_pad, 0), (0, 0)))
    ys, wkv_s, shift_s, conv_s = _run_group(xs, list(wkv0), list(shift0), list(conv0), pp,
                                            tm=nb * dt, nb=SMALL_STREAMS, tt=dt, chunk=dt, proj_streams=SMALL_STREAMS)
    yp, wkv_p, shift_p, conv_p = _run_group(
        x_prompt, [s[db:] for s in wkv_s], [s[db:] for s in shift_s], [s[db:] for s in conv_s],
        pp, tm=512, nb=PROMPT_STREAMS, tt=PROMPT_TILE, chunk=WKV_CHUNK, proj_streams=PROMPT_STREAMS)

    stack = lambda xs_, f: jnp.stack([f(x) for x in xs_])
    return (yp, ys[:db],
            stack(wkv_p, lambda s: s),
            stack(shift_p, lambda s: s[:, 0, :P_RWKV]),
            stack(conv_p, lambda s: s[:, hist_pad:]),
            stack(wkv_s, lambda s: s[:db]),
            stack(shift_s, lambda s: s[:db, 0, :P_RWKV]),
            stack(conv_s, lambda s: s[:db, hist_pad:]))
```
